```python
import math
import jax, jax.numpy as jnp
from jax import lax
import numpy as np

D_MODEL = 2048
BATCH = 1
SEQ = 8192
DEPTH = 2

CHUNK = 64
GDN_HEADS = 8
GDN_HEAD_DIM = 128
GDN_DIM = GDN_HEADS * GDN_HEAD_DIM
SHORT_CONV = 4
RET_HEADS = 4
RET_QK_DIM = 128
RET_V_DIM = 256
RET_QK = RET_HEADS * RET_QK_DIM
RET_V = RET_HEADS * RET_V_DIM
RET_DECAY_BASE = 5.0
ROPE_BASE = 10000.0
CONV_DIM = D_MODEL // 2
CONV_WIDTH = 31
N_BRANCH = 3
D_FF = 4 * D_MODEL
NORM_EPS = 1e-6
MAX_STREAM_OFFSET = 4096

IN_SIZES = (GDN_DIM, GDN_DIM, GDN_DIM, GDN_DIM, GDN_HEADS, GDN_HEADS,
            RET_QK, RET_QK, RET_V, RET_V, 2 * CONV_DIM, N_BRANCH * D_MODEL)
IN_WIDTH = 4 * GDN_DIM + 2 * GDN_HEADS + 2 * RET_QK + 2 * RET_V + 2 * CONV_DIM + N_BRANCH * D_MODEL

kernel_name = "hybrid_gdn_retnet_conformer_adaln"


def _split_points(sizes):
    pts, acc = [], 0
    for s in sizes[:-1]:
        acc += s
        pts.append(acc)
    return pts


def rms_norm(x, w, eps=NORM_EPS):
    xf = x.astype(jnp.float32)
    y = xf * lax.rsqrt(jnp.mean(xf * xf, axis=-1, keepdims=True) + eps)
    return (y * w.astype(jnp.float32)).astype(x.dtype)


def layer_norm_f32(x, eps=1e-5):
    mu = jnp.mean(x, axis=-1, keepdims=True)
    var = jnp.mean(jnp.square(x - mu), axis=-1, keepdims=True)
    return (x - mu) * lax.rsqrt(var + eps)


def l2_norm(x, eps=1e-6):
    return x * lax.rsqrt(jnp.sum(x * x, axis=-1, keepdims=True) + eps)


def causal_depthwise_conv(x, w):
    k_width, ch = w.shape
    return lax.conv_general_dilated(
        x, w[:, None, :].astype(x.dtype), window_strides=(1,),
        padding=[(k_width - 1, 0)], dimension_numbers=('NWC', 'WIO', 'NWC'),
        feature_group_count=ch)


def rotary(x, positions):
    half = x.shape[-1] // 2
    inv_freq = ROPE_BASE ** (-jnp.arange(half, dtype=jnp.float32) / half)
    ang = positions.astype(jnp.float32)[..., None] * inv_freq
    cos = jnp.cos(ang)[:, :, None, :]
    sin = jnp.sin(ang)[:, :, None, :]
    x1, x2 = x[..., :half], x[..., half:]
    return jnp.concatenate([x1 * cos - x2 * sin, x1 * sin + x2 * cos], axis=-1)


def _to_chunks(a):
    b, t, h = a.shape[:3]
    a = a.reshape((b, t // CHUNK, CHUNK, h) + a.shape[3:])
    return jnp.moveaxis(a, 3, 1)


def _from_chunks(a):
    b, h, n, c, d = a.shape
    return jnp.transpose(a, (0, 2, 3, 1, 4)).reshape(b, n * c, h, d)


def gated_delta_rule(q, k, v, g, beta):
    bsz, _, h, dk = q.shape
    dv = v.shape[-1]
    q = _to_chunks(q * dk ** -0.5)
    k = _to_chunks(k)
    v = _to_chunks(v)
    g = jnp.cumsum(_to_chunks(g), axis=-1)
    beta = _to_chunks(beta)
    idx = jnp.arange(CHUNK)
    causal = idx[:, None] >= idx[None, :]
    strict = idx[:, None] > idx[None, :]
    decay = jnp.exp(jnp.where(causal, g[..., :, None] - g[..., None, :], -jnp.inf))
    k_beta = k * beta[..., None]
    v_beta = v * beta[..., None]
    lower = jnp.where(strict, jnp.einsum('bhncd,bhnsd->bhncs', k_beta, k) * decay, 0.0)
    eye = jnp.eye(CHUNK, dtype=jnp.float32)
    t_inv = lax.linalg.triangular_solve(eye + lower, jnp.broadcast_to(eye, lower.shape),
                                        left_side=True, lower=True)
    u = jnp.einsum('bhncs,bhnse->bhnce', t_inv, v_beta)
    w = jnp.einsum('bhncs,bhnsd->bhncd', t_inv, k_beta * jnp.exp(g)[..., None])
    attn = jnp.where(causal, jnp.einsum('bhncd,bhnsd->bhncs', q, k) * decay, 0.0)
    q_dec = q * jnp.exp(g)[..., None]
    k_dec = k * jnp.exp(g[..., -1:] - g)[..., None]
    g_last = jnp.exp(g[..., -1])

    def step(state, xs):
        q_i, k_i, u_i, w_i, a_i, gl_i = xs
        v_new = u_i - jnp.einsum('bhcd,bhde->bhce', w_i, state)
        o = jnp.einsum('bhcd,bhde->bhce', q_i, state) + jnp.einsum('bhcs,bhse->bhce', a_i, v_new)
        state = state * gl_i[..., None, None] + jnp.einsum('bhcd,bhce->bhde', k_i, v_new)
        return state, o

    xs = tuple(jnp.moveaxis(a, 2, 0) for a in (q_dec, k_dec, u, w, attn, g_last))
    s0 = jnp.zeros((bsz, h, dk, dv), jnp.float32)
    _, o = lax.scan(step, s0, xs)
    return _from_chunks(jnp.moveaxis(o, 0, 2))


def chunkwise_retention(q, k, v):
    bsz, _, h, dk = q.shape
    dv = v.shape[-1]
    log_gamma = jnp.log(1.0 - jnp.exp2(-RET_DECAY_BASE - jnp.arange(h, dtype=jnp.float32)))
    q = _to_chunks(q)
    k = _to_chunks(k * dk ** -0.5)
    v = _to_chunks(v)
    idx = jnp.arange(CHUNK, dtype=jnp.float32)
    rel = idx[:, None] - idx[None, :]
    dmask = jnp.where(rel >= 0, jnp.exp(jnp.maximum(rel, 0.0) * log_gamma[:, None, None]), 0.0)
    scores = jnp.einsum('bhncd,bhnsd->bhncs', q, k) * dmask[None, :, None]
    o_intra = jnp.einsum('bhncs,bhnse->bhnce', scores, v)
    inner = jnp.exp((idx + 1.0) * log_gamma[:, None])
    tail = jnp.exp((CHUNK - 1.0 - idx) * log_gamma[:, None])
    chunk_decay = jnp.exp(CHUNK * log_gamma)
    q_in = q * inner[None, :, None, :, None]
    k_tail = k * tail[None, :, None, :, None]

    def step(state, xs):
        q_i, k_i, v_i = xs
        o = jnp.einsum('bhcd,bhde->bhce', q_i, state)
        state = state * chunk_decay[None, :, None, None] + jnp.einsum('bhcd,bhce->bhde', k_i, v_i)
        return state, o

    xs = tuple(jnp.moveaxis(a, 2, 0) for a in (q_in, k_tail, v))
    r0 = jnp.zeros((bsz, h, dk, dv), jnp.float32)
    _, o_inter = lax.scan(step, r0, xs)
    return _from_chunks(o_intra + jnp.moveaxis(o_inter, 0, 2))


def hybrid_mixer(h, positions, w_in, conv_qkv_w, a_log, dt_bias, gdn_norm_w,
                 conv_dw_w, conv_dw_b, conv_ln_w, conv_ln_b,
                 w_branch_a, w_branch_b, w_branch_c, w_out):
    bsz, t, _ = h.shape
    dt = h.dtype
    proj = h @ w_in
    (a_q, a_k, a_v, a_z, a_beta, a_alpha,
     b_q, b_k, b_v, b_g, c_glu, gate_raw) = jnp.split(proj, _split_points(IN_SIZES), axis=-1)

    qkv = jax.nn.silu(causal_depthwise_conv(jnp.concatenate([a_q, a_k, a_v], axis=-1), conv_qkv_w))
    a_q, a_k, a_v = jnp.split(qkv.astype(jnp.float32), 3, axis=-1)
    heads = lambda z, n, d: z.reshape(bsz, t, n, d)
    gq = l2_norm(heads(a_q, GDN_HEADS, GDN_HEAD_DIM))
    gk = l2_norm(heads(a_k, GDN_HEADS, GDN_HEAD_DIM))
    gv = heads(a_v, GDN_HEADS, GDN_HEAD_DIM)
    beta = jax.nn.sigmoid(a_beta.astype(jnp.float32))
    g = -jnp.exp(a_log.astype(jnp.float32)) * jax.nn.softplus(a_alpha.astype(jnp.float32) + dt_bias.astype(jnp.float32))
    o_a = gated_delta_rule(gq, gk, gv, g, beta)
    o_a = o_a * lax.rsqrt(jnp.mean(o_a * o_a, axis=-1, keepdims=True) + NORM_EPS) * gdn_norm_w.astype(jnp.float32)
    o_a = o_a * jax.nn.silu(heads(a_z.astype(jnp.float32), GDN_HEADS, GDN_HEAD_DIM))
    y_a = o_a.reshape(bsz, t, GDN_DIM).astype(dt) @ w_branch_a

    rq = rotary(heads(b_q.astype(jnp.float32), RET_HEADS, RET_QK_DIM), positions)
    rk = rotary(heads(b_k.astype(jnp.float32), RET_HEADS, RET_QK_DIM), positions)
    rv = heads(b_v.astype(jnp.float32), RET_HEADS, RET_V_DIM)
    o_b = layer_norm_f32(chunkwise_retention(rq, rk, rv))
    o_b = o_b.reshape(bsz, t, RET_V) * jax.nn.silu(b_g.astype(jnp.float32))
    y_b = o_b.astype(dt) @ w_branch_b

    c_a, c_b = jnp.split(c_glu, 2, axis=-1)
    u = c_a * jax.nn.sigmoid(c_b)
    u = causal_depthwise_conv(u, conv_dw_w) + conv_dw_b.astype(dt)
    u = layer_norm_f32(u.astype(jnp.float32)) * conv_ln_w.astype(jnp.float32) + conv_ln_b.astype(jnp.float32)
    y_c = jax.nn.silu(u).astype(dt) @ w_branch_c

    gates = jax.nn.sigmoid(gate_raw.reshape(bsz, t, N_BRANCH, D_MODEL))
    merged = gates[:, :, 0] * y_a + gates[:, :, 1] * y_b + gates[:, :, 2] * y_c
    return merged @ w_out


def setup_inputs(seed: int = 0) -> dict:
    key = jax.random.key(seed)
    ks = jax.random.split(key, 24)
    nrm = lambda k, shape, std: jax.random.normal(k, shape, jnp.float32) * std
    x = nrm(ks[0], (BATCH, SEQ, D_MODEL), 1.0)
    c = nrm(ks[1], (BATCH, D_MODEL), 1.0)
    start = jax.random.randint(ks[2], (BATCH, 1), 0, MAX_STREAM_OFFSET, dtype=jnp.int32)
    positions = (start + jnp.arange(SEQ, dtype=jnp.int32)[None, :]).astype(jnp.int32)
    w_ada = nrm(ks[3], (DEPTH, D_MODEL, 6 * D_MODEL), 0.5 * D_MODEL ** -0.5)
    b_ada = nrm(ks[4], (DEPTH, 6 * D_MODEL), 0.01)
    norm_mix_w = 1.0 + nrm(ks[5], (DEPTH, D_MODEL), 0.02)
    norm_mlp_w = 1.0 + nrm(ks[6], (DEPTH, D_MODEL), 0.02)
    w_in = nrm(ks[7], (DEPTH, D_MODEL, IN_WIDTH), D_MODEL ** -0.5)
    conv_qkv_w = nrm(ks[8], (DEPTH, SHORT_CONV, 3 * GDN_DIM), SHORT_CONV ** -0.5)
    gdn_a_log = jnp.log(jax.random.uniform(ks[9], (DEPTH, GDN_HEADS), jnp.float32, 1.0, 16.0))
    dt0 = jnp.exp(jax.random.uniform(ks[10], (DEPTH, GDN_HEADS), jnp.float32, math.log(1e-3), math.log(1e-1)))
    gdn_dt_bias = dt0 + jnp.log(-jnp.expm1(-dt0))
    gdn_norm_w = 1.0 + nrm(ks[11], (DEPTH, GDN_HEAD_DIM), 0.02)
    conv_dw_w = nrm(ks[12], (DEPTH, CONV_WIDTH, CONV_DIM), CONV_WIDTH ** -0.5)
    conv_dw_b = nrm(ks[13], (DEPTH, CONV_DIM), 0.01)
    conv_ln_w = 1.0 + nrm(ks[14], (DEPTH, CONV_DIM), 0.02)
    conv_ln_b = nrm(ks[15], (DEPTH, CONV_DIM), 0.01)
    w_branch_a = nrm(ks[16], (DEPTH, GDN_DIM, D_MODEL), GDN_DIM ** -0.5)
    w_branch_b = nrm(ks[17], (DEPTH, RET_V, D_MODEL), RET_V ** -0.5)
    w_branch_c = nrm(ks[18], (DEPTH, CONV_DIM, D_MODEL), CONV_DIM ** -0.5)
    w_out = nrm(ks[19], (DEPTH, D_MODEL, D_MODEL), D_MODEL ** -0.5)
    w_mlp_in = nrm(ks[20], (DEPTH, D_MODEL, D_FF), D_MODEL ** -0.5)
    w_mlp_out = nrm(ks[21], (DEPTH, D_FF, D_MODEL), D_FF ** -0.5)
    final_norm_w = 1.0 + nrm(ks[22], (D_MODEL,), 0.02)
    return {"x": x, "c": c, "positions": positions, "w_ada": w_ada, "b_ada": b_ada,
            "norm_mix_w": norm_mix_w, "norm_mlp_w": norm_mlp_w, "w_in": w_in,
            "conv_qkv_w": conv_qkv_w, "gdn_a_log": gdn_a_log, "gdn_dt_bias": gdn_dt_bias,
            "gdn_norm_w": gdn_norm_w, "conv_dw_w": conv_dw_w, "conv_dw_b": conv_dw_b,
            "conv_ln_w": conv_ln_w, "conv_ln_b": conv_ln_b, "w_branch_a": w_branch_a,
            "w_branch_b": w_branch_b, "w_branch_c": w_branch_c, "w_out": w_out,
            "w_mlp_in": w_mlp_in, "w_mlp_out": w_mlp_out, "final_norm_w": final_norm_w}


def reference(x, c, positions, w_ada, b_ada, norm_mix_w, norm_mlp_w, w_in, conv_qkv_w,
              gdn_a_log, gdn_dt_bias, gdn_norm_w, conv_dw_w, conv_dw_b, conv_ln_w, conv_ln_b,
              w_branch_a, w_branch_b, w_branch_c, w_out, w_mlp_in, w_mlp_out, final_norm_w):
    c_act = jax.nn.silu(c)
    for l in range(DEPTH):
        mod = c_act @ w_ada[l] + b_ada[l]
        shift1, scale1, gate1, shift2, scale2, gate2 = [m[:, None, :] for m in jnp.split(mod, 6, axis=-1)]
        h = rms_norm(x, norm_mix_w[l]) * (1.0 + scale1) + shift1
        x = x + gate1 * hybrid_mixer(h, positions, w_in[l], conv_qkv_w[l], gdn_a_log[l], gdn_dt_bias[l],
                                     gdn_norm_w[l], conv_dw_w[l], conv_dw_b[l], conv_ln_w[l], conv_ln_b[l],
                                     w_branch_a[l], w_branch_b[l], w_branch_c[l], w_out[l])
        h = rms_norm(x, norm_mlp_w[l]) * (1.0 + scale2) + shift2
        x = x + gate2 * (jnp.square(jax.nn.relu(h @ w_mlp_in[l])) @ w_mlp_out[l])
    return rms_norm(x, final_norm_w)
```

```python
import functools
import math

import jax
import jax.numpy as jnp
from jax import lax
from jax.experimental import pallas as pl
from jax.experimental.pallas import tpu as pltpu

F32 = jnp.float32
BF16 = jnp.bfloat16

D_MODEL = 2048
DEPTH = 2
CHUNK = 64
GDN_HEADS = 8
GDN_HEAD_DIM = 128
GDN_DIM = GDN_HEADS * GDN_HEAD_DIM
SHORT_CONV = 4
RET_HEADS = 4
RET_QK_DIM = 128
RET_V_DIM = 256
RET_QK = RET_HEADS * RET_QK_DIM
RET_V = RET_HEADS * RET_V_DIM
RET_DECAY_BASE = 5.0
ROPE_BASE = 10000.0
CONV_DIM = D_MODEL // 2
CONV_WIDTH = 31
N_BRANCH = 3
D_FF = 4 * D_MODEL
NORM_EPS = 1e-6
LN_EPS = 1e-5
L2_EPS = 1e-6

OFF_GDN = 0
OFF_BA = 4 * GDN_DIM
OFF_REST = OFF_BA + 2 * GDN_HEADS
REST_W = 2 * RET_QK + 2 * RET_V + 2 * CONV_DIM + N_BRANCH * D_MODEL
R_Q, R_K = 0, RET_QK
R_V = 2 * RET_QK
R_G = R_V + RET_V
R_GLU = R_G + RET_V
R_GATE = R_GLU + 2 * CONV_DIM

LANES = 128
SUBLANES = 8
PAIR = 2 * CHUNK
VMEM_LIMIT_BYTES = 56 * 1024 * 1024
TM = 512
TN = 1024
TB_GDN = 512
TB_RET = 256
TB_CONV = 256
TB_NORM = 512
HALO = 32


def _params(*sem):
    return pltpu.CompilerParams(dimension_semantics=sem, vmem_limit_bytes=VMEM_LIMIT_BYTES)


def _dot(a, b):
    return jnp.dot(a, b, preferred_element_type=F32)


def _dot_nt(a, b):
    return lax.dot_general(a, b, (((1,), (1,)), ((), ())), preferred_element_type=F32)


def _split_bf16(a):
    hi = a.astype(BF16)
    lo = (a - hi.astype(F32)).astype(BF16)
    return hi, lo


def _dot3(a, b):
    ah, al = _split_bf16(a)
    bh, bl = _split_bf16(b)
    return _dot(jnp.concatenate([ah, ah, al], axis=1), jnp.concatenate([bh, bl, bh], axis=0))


def _dot_exact_lhs(m_bf16, g):
    g1 = g.astype(BF16)
    r1 = g - g1.astype(F32)
    g2 = r1.astype(BF16)
    g3 = (r1 - g2.astype(F32)).astype(BF16)
    return _dot(m_bf16, g1) + _dot(m_bf16, g2) + _dot(m_bf16, g3)


def _silu(x):
    return x * jax.nn.sigmoid(x)


def _mod_kernel(c_ref, w_ref, b_ref, o_ref):
    c = c_ref[...]
    o_ref[0] = jnp.dot(_silu(c), w_ref[0], preferred_element_type=F32,
                       precision=lax.Precision.HIGHEST) + b_ref[0]


def _adaln_mod(c, w_ada, b_ada):
    depth, d, n = w_ada.shape
    tn = 1024
    c8 = jnp.broadcast_to(c, (SUBLANES, d))
    out = pl.pallas_call(
        _mod_kernel,
        grid=(depth, n // tn),
        in_specs=[pl.BlockSpec((SUBLANES, d), lambda l, j: (0, 0)),
                  pl.BlockSpec((1, d, tn), lambda l, j: (l, 0, j)),
                  pl.BlockSpec((1, 1, tn), lambda l, j: (l, 0, j))],
        out_specs=pl.BlockSpec((1, SUBLANES, tn), lambda l, j: (l, 0, j)),
        out_shape=jax.ShapeDtypeStruct((depth, SUBLANES, n), F32),
        compiler_params=_params("arbitrary", "arbitrary"),
    )(c8, w_ada, b_ada.reshape(depth, 1, n))
    return out[:, 0, :].reshape(depth, 6, d)


def _norm_mod_kernel(x_ref, w_ref, mod_ref, o_ref, *, shift_row, scale_row):
    x = x_ref[...]
    y = x * lax.rsqrt(jnp.mean(x * x, axis=-1, keepdims=True) + NORM_EPS) * w_ref[...]
    y = y * (1.0 + mod_ref[scale_row:scale_row + 1, :]) + mod_ref[shift_row:shift_row + 1, :]
    o_ref[...] = y.astype(o_ref.dtype)


def _norm_mod(x, w, mod, shift_row, scale_row):
    t, d = x.shape
    return pl.pallas_call(
        functools.partial(_norm_mod_kernel, shift_row=shift_row, scale_row=scale_row),
        grid=(t // TB_NORM,),
        in_specs=[pl.BlockSpec((TB_NORM, d), lambda i: (i, 0)),
                  pl.BlockSpec((1, d), lambda i: (0, 0)),
                  pl.BlockSpec((6, d), lambda i: (0, 0))],
        out_specs=pl.BlockSpec((TB_NORM, d), lambda i: (i, 0)),
        out_shape=jax.ShapeDtypeStruct((t, d), BF16),
        compiler_params=_params("arbitrary"),
    )(x, w.reshape(1, d), mod)


def _final_norm_kernel(x_ref, w_ref, o_ref):
    x = x_ref[...]
    o_ref[...] = x * lax.rsqrt(jnp.mean(x * x, axis=-1, keepdims=True) + NORM_EPS) * w_ref[...]


def _final_norm(x, w):
    t, d = x.shape
    return pl.pallas_call(
        _final_norm_kernel,
        grid=(t // TB_NORM,),
        in_specs=[pl.BlockSpec((TB_NORM, d), lambda i: (i, 0)),
                  pl.BlockSpec((1, d), lambda i: (0, 0))],
        out_specs=pl.BlockSpec((TB_NORM, d), lambda i: (i, 0)),
        out_shape=jax.ShapeDtypeStruct((t, d), F32),
        compiler_params=_params("arbitrary"),
    )(x, w.reshape(1, d))


def _mm_kernel(a_ref, w_ref, o_ref, wb_ref):
    @pl.when(pl.program_id(1) == 0)
    def _():
        wb_ref[...] = w_ref[...].astype(BF16)
    o_ref[...] = _dot(a_ref[...], wb_ref[...]).astype(o_ref.dtype)


def _mm_relu2_kernel(a_ref, w_ref, o_ref, wb_ref):
    @pl.when(pl.program_id(1) == 0)
    def _():
        wb_ref[...] = w_ref[...].astype(BF16)
    y = jnp.maximum(_dot(a_ref[...], wb_ref[...]), 0.0)
    o_ref[...] = (y * y).astype(o_ref.dtype)


def _mm_residual_kernel(a_ref, w_ref, x_ref, g_ref, o_ref, wb_ref):
    @pl.when(pl.program_id(1) == 0)
    def _():
        wb_ref[...] = w_ref[...].astype(BF16)
    o_ref[...] = x_ref[...] + g_ref[...] * _dot(a_ref[...], wb_ref[...])


def _matmul(a, w, layer, *, col0=0, ncols=None, out_dtype=F32, relu2=False):
    t, k = a.shape
    ncols = w.shape[2] - col0 if ncols is None else ncols
    tn = min(TN, ncols)
    assert ncols % tn == 0 and col0 % tn == 0 and t % TM == 0
    cb = col0 // tn
    return pl.pallas_call(
        _mm_relu2_kernel if relu2 else _mm_kernel,
        grid=(ncols // tn, t // TM),
        in_specs=[pl.BlockSpec((TM, k), lambda j, i: (i, 0)),
                  pl.BlockSpec((None, k, tn), lambda j, i: (layer, 0, j + cb))],
        out_specs=pl.BlockSpec((TM, tn), lambda j, i: (i, j)),
        out_shape=jax.ShapeDtypeStruct((t, ncols), out_dtype),
        scratch_shapes=[pltpu.VMEM((k, tn), BF16)],
        compiler_params=_params("arbitrary", "arbitrary"),
    )(a, w)


def _matmul_residual(a, w, layer, x, gate):
    t, k = a.shape
    n = w.shape[2]
    tn = min(TN, n)
    return pl.pallas_call(
        _mm_residual_kernel,
        grid=(n // tn, t // TM),
        in_specs=[pl.BlockSpec((TM, k), lambda j, i: (i, 0)),
                  pl.BlockSpec((None, k, tn), lambda j, i: (layer, 0, j)),
                  pl.BlockSpec((TM, tn), lambda j, i: (i, j)),
                  pl.BlockSpec((1, tn), lambda j, i: (0, j))],
        out_specs=pl.BlockSpec((TM, tn), lambda j, i: (i, j)),
        out_shape=jax.ShapeDtypeStruct((t, n), F32),
        scratch_shapes=[pltpu.VMEM((k, tn), BF16)],
        compiler_params=_params("arbitrary", "arbitrary"),
    )(a, w, x, gate)


def _mlp_out_kernel(a_ref, w_ref, x_ref, g_ref, o_ref, acc_ref):
    kk = pl.program_id(2)

    @pl.when(kk == 0)
    def _():
        acc_ref[...] = jnp.zeros_like(acc_ref)
    acc_ref[...] += _dot(a_ref[...], w_ref[...].astype(BF16))

    @pl.when(kk == pl.num_programs(2) - 1)
    def _():
        o_ref[...] = x_ref[...] + g_ref[...] * acc_ref[...]


def _mlp_out(a, w, layer, x, gate):
    t, k = a.shape
    n = w.shape[2]
    tm, tn, tk = 1024, 1024, 1024
    return pl.pallas_call(
        _mlp_out_kernel,
        grid=(t // tm, n // tn, k // tk),
        in_specs=[pl.BlockSpec((tm, tk), lambda i, j, kk: (i, kk)),
                  pl.BlockSpec((None, tk, tn), lambda i, j, kk: (layer, kk, j)),
                  pl.BlockSpec((tm, tn), lambda i, j, kk: (i, j)),
                  pl.BlockSpec((1, tn), lambda i, j, kk: (0, j))],
        out_specs=pl.BlockSpec((tm, tn), lambda i, j, kk: (i, j)),
        out_shape=jax.ShapeDtypeStruct((t, n), F32),
        scratch_shapes=[pltpu.VMEM((tm, tn), F32)],
        compiler_params=_params("arbitrary", "arbitrary", "arbitrary"),
    )(a, w, x, gate)


def _gdn_gate_kernel(h_ref, w_ref, alog_ref, dtb_ref, bg_ref, bgt_ref):
    tb = h_ref.shape[0]
    ba = _dot(h_ref[...], w_ref[...].astype(BF16))
    beta = jax.nn.sigmoid(ba)
    xs = ba + dtb_ref[...]
    softplus = jnp.maximum(xs, 0.0) + jnp.log1p(jnp.exp(-jnp.abs(xs)))
    g = -jnp.exp(alog_ref[...]) * softplus
    ii = lax.broadcasted_iota(jnp.int32, (tb, tb), 0)
    jj = lax.broadcasted_iota(jnp.int32, (tb, tb), 1)
    shift = int(math.log2(CHUNK))
    same = jnp.right_shift(ii, shift) == jnp.right_shift(jj, shift)
    tri = jnp.where(same & (ii >= jj), 1.0, 0.0).astype(BF16)
    blk = jnp.where(same, 1.0, 0.0).astype(BF16)
    gc = _dot_exact_lhs(tri, g)
    gl = _dot_exact_lhs(blk, g)
    lane = lax.broadcasted_iota(jnp.int32, (tb, LANES), 1)
    out = jnp.where(lane < GDN_HEADS, beta,
                    jnp.where(lane < 2 * GDN_HEADS, gc, pltpu.roll(gl, GDN_HEADS, axis=1)))
    bg_ref[...] = out
    bgt_ref[...] = out.T


def _gdn_gates(h, w_ba, a_log, dt_bias):
    t, d = h.shape
    tb = 512
    pad = LANES - 2 * GDN_HEADS
    w_pad = jnp.pad(w_ba, ((0, 0), (0, pad)))
    lead = jnp.zeros((GDN_HEADS,), F32)
    tail = jnp.zeros((LANES - 2 * GDN_HEADS,), F32)
    alog = jnp.concatenate([lead, a_log, tail]).reshape(1, LANES)
    dtb = jnp.concatenate([lead, dt_bias, tail]).reshape(1, LANES)
    return pl.pallas_call(
        _gdn_gate_kernel,
        grid=(t // tb,),
        in_specs=[pl.BlockSpec((tb, d), lambda i: (i, 0)),
                  pl.BlockSpec((d, LANES), lambda i: (0, 0)),
                  pl.BlockSpec((1, LANES), lambda i: (0, 0)),
                  pl.BlockSpec((1, LANES), lambda i: (0, 0))],
        out_specs=[pl.BlockSpec((tb, LANES), lambda i: (i, 0)),
                   pl.BlockSpec((LANES, tb), lambda i: (0, i))],
        out_shape=[jax.ShapeDtypeStruct((t, LANES), F32),
                   jax.ShapeDtypeStruct((LANES, t), F32)],
        compiler_params=_params("arbitrary"),
    )(h, w_pad, alog, dtb)


def _tri_inverse(low):
    n = low.shape[0]
    eye = jnp.where(lax.broadcasted_iota(jnp.int32, (n, n), 0) == lax.broadcasted_iota(jnp.int32, (n, n), 1),
                    1.0, 0.0)
    x = eye - low
    p = _dot3(low, low)
    steps = int(math.log2(CHUNK)) - 1
    for s in range(steps):
        x = x + _dot3(x, p)
        if s + 1 < steps:
            p = _dot3(p, p)
    return x


def _gdn_kernel(q_ref, k_ref, v_ref, z_ref, bg_ref, bgt_ref, cwq_ref, cwk_ref, cwv_ref, nw_ref,
                o_ref, xpad_ref, s_ref):
    tb = q_ref.shape[0]
    head = pl.program_id(0)

    @pl.when(pl.program_id(1) == 0)
    def _():
        xpad_ref[:, 0:SUBLANES, :] = jnp.zeros((3, SUBLANES, GDN_HEAD_DIM), F32)
        s_ref[...] = jnp.zeros_like(s_ref)

    xpad_ref[0, SUBLANES:, :] = q_ref[...]
    xpad_ref[1, SUBLANES:, :] = k_ref[...]
    xpad_ref[2, SUBLANES:, :] = v_ref[...]

    def conv(i, cw_ref):
        base = SUBLANES - (SHORT_CONV - 1)
        acc = cw_ref[0:1, :] * xpad_ref[i, base:base + tb, :]
        for j in range(1, SHORT_CONV):
            acc = acc + cw_ref[j:j + 1, :] * xpad_ref[i, base + j:base + j + tb, :]
        return _silu(acc)

    q = conv(0, cwq_ref)
    k = conv(1, cwk_ref)
    v = conv(2, cwv_ref)
    xpad_ref[:, 0:SUBLANES, :] = xpad_ref[:, tb:tb + SUBLANES, :]

    q = q * lax.rsqrt(jnp.sum(q * q, axis=-1, keepdims=True) + L2_EPS) * (GDN_HEAD_DIM ** -0.5)
    k = k * lax.rsqrt(jnp.sum(k * k, axis=-1, keepdims=True) + L2_EPS)

    bg = bg_ref[...]
    lane = lax.broadcasted_iota(jnp.int32, bg.shape, 1)
    pick = lambda off: jnp.sum(jnp.where(lane == head + off, bg, 0.0), axis=1, keepdims=True)
    beta, gc, gl = pick(0), pick(GDN_HEADS), pick(2 * GDN_HEADS)
    bgt = bgt_ref[...]
    sub = lax.broadcasted_iota(jnp.int32, bgt.shape, 0)
    gc_row = jnp.sum(jnp.where(sub == head, bgt, 0.0), axis=0, keepdims=True)

    eg = jnp.exp(gc)
    kb = k * beta
    rhs = jnp.concatenate([v * beta, kb * eg], axis=1).astype(BF16)
    q_dec = (q * eg).astype(BF16)
    k_dec = k * jnp.exp(gl - gc)
    egl = jnp.broadcast_to(jnp.exp(gl), (tb, GDN_HEAD_DIM))
    kb16 = kb.astype(BF16)
    k16 = k.astype(BF16)
    q16 = q.astype(BF16)

    ii = lax.broadcasted_iota(jnp.int32, (PAIR, PAIR), 0)
    jj = lax.broadcasted_iota(jnp.int32, (PAIR, PAIR), 1)
    same = (ii >= CHUNK) == (jj >= CHUNK)
    causal = same & (ii >= jj)
    strict = same & (ii > jj)
    row_lo = lax.broadcasted_iota(jnp.int32, (PAIR, GDN_HEAD_DIM), 0) < CHUNK

    state = s_ref[...]
    outs = []
    for p in range(tb // PAIR):
        r0 = p * PAIR
        rs = slice(r0, r0 + PAIR)
        decay = jnp.exp(jnp.where(causal, gc[rs] - gc_row[:, rs], -jnp.inf))
        low = jnp.where(strict, _dot_nt(kb16[rs], k16[rs]) * decay, 0.0)
        t_inv = _tri_inverse(low).astype(BF16)
        uw = _dot(t_inv, rhs[rs])
        u, w = uw[:, :GDN_HEAD_DIM], uw[:, GDN_HEAD_DIM:].astype(BF16)
        attn = (_dot_nt(q16[rs], k16[rs]) * decay).astype(BF16)
        kd = k_dec[rs]
        kdt = (jnp.where(row_lo, kd, 0.0).T.astype(BF16), jnp.where(row_lo, 0.0, kd).T.astype(BF16))
        v_new = jnp.zeros((PAIR, GDN_HEAD_DIM), F32)
        o_inter = []
        for c in range(PAIR // CHUNK):
            cs = slice(c * CHUNK, (c + 1) * CHUNK)
            s16 = state.astype(BF16)
            vc = u[cs] - _dot(w[cs], s16)
            o_inter.append(_dot(q_dec[r0 + c * CHUNK:r0 + (c + 1) * CHUNK], s16))
            v_new = jnp.concatenate([vc, v_new[CHUNK:]], axis=0) if c == 0 else \
                jnp.concatenate([v_new[:CHUNK], vc], axis=0)
            state = state * egl[r0 + c * CHUNK:r0 + c * CHUNK + 1, :] + _dot(kdt[c], v_new.astype(BF16))
        outs.append(jnp.concatenate(o_inter, axis=0) + _dot(attn, v_new.astype(BF16)))
    s_ref[...] = state

    o = jnp.concatenate(outs, axis=0)
    o = o * lax.rsqrt(jnp.mean(o * o, axis=-1, keepdims=True) + NORM_EPS) * nw_ref[...]
    o_ref[...] = (o * _silu(z_ref[...])).astype(o_ref.dtype)


def _gdn(proj_a, bg, bgt, conv_w, norm_w):
    t = proj_a.shape[0]
    tb = TB_GDN
    hd, nh = GDN_HEAD_DIM, GDN_HEADS
    col = lambda base: pl.BlockSpec((tb, hd), lambda h, i: (i, base * nh + h))
    cw = lambda base: pl.BlockSpec((SHORT_CONV, hd), lambda h, i: (0, base * nh + h))
    return pl.pallas_call(
        _gdn_kernel,
        grid=(nh, t // tb),
        in_specs=[col(0), col(1), col(2), col(3),
                  pl.BlockSpec((tb, LANES), lambda h, i: (i, 0)),
                  pl.BlockSpec((SUBLANES, tb), lambda h, i: (1, i)),
                  cw(0), cw(1), cw(2),
                  pl.BlockSpec((1, hd), lambda h, i: (0, 0))],
        out_specs=pl.BlockSpec((tb, hd), lambda h, i: (i, h)),
        out_shape=jax.ShapeDtypeStruct((t, GDN_DIM), BF16),
        scratch_shapes=[pltpu.VMEM((3, tb + SUBLANES, hd), F32),
                        pltpu.VMEM((hd, hd), F32)],
        compiler_params=_params("arbitrary", "arbitrary"),
    )(proj_a, proj_a, proj_a, proj_a, bg, bgt, conv_w, conv_w, conv_w, norm_w.reshape(1, hd))


def _rope_kernel(pos_ref, cos_ref, sin_ref):
    half = RET_QK_DIM // 2
    lane = lax.broadcasted_iota(jnp.int32, (1, RET_QK_DIM), 1)
    idx = jnp.where(lane < half, lane, lane - half).astype(F32)
    inv_freq = jnp.exp(idx * (-math.log(ROPE_BASE) / half))
    ang = pos_ref[...].astype(F32) * inv_freq
    cos_ref[...] = jnp.cos(ang)
    sin_ref[...] = jnp.where(lane < half, -1.0, 1.0) * jnp.sin(ang)


def _rope_tables(positions):
    t = positions.shape[0]
    tb = 1024
    return pl.pallas_call(
        _rope_kernel,
        grid=(t // tb,),
        in_specs=[pl.BlockSpec((tb, 1), lambda i: (i, 0))],
        out_specs=[pl.BlockSpec((tb, RET_QK_DIM), lambda i: (i, 0))] * 2,
        out_shape=[jax.ShapeDtypeStruct((t, RET_QK_DIM), F32)] * 2,
        compiler_params=_params("arbitrary"),
    )(positions.reshape(t, 1))


def _ret_kernel(q_ref, k_ref, v_ref, g_ref, cos_ref, sin_ref, o_ref, r_ref):
    tb = q_ref.shape[0]
    head = pl.program_id(0)

    @pl.when(pl.program_id(1) == 0)
    def _():
        r_ref[...] = jnp.zeros_like(r_ref)

    hf = jnp.full((1, 1), head, jnp.int32).astype(F32)
    log_gamma = jnp.log(1.0 - jnp.exp2(-RET_DECAY_BASE - hf))
    cos, sin = cos_ref[...], sin_ref[...]
    half = RET_QK_DIM // 2
    rope = lambda x: x * cos + pltpu.roll(x, half, axis=1) * sin
    q = rope(q_ref[...])
    k = rope(k_ref[...]) * (RET_QK_DIM ** -0.5)
    v16 = v_ref[...].astype(BF16)

    ii = lax.broadcasted_iota(jnp.int32, (tb, tb), 0)
    jj = lax.broadcasted_iota(jnp.int32, (tb, tb), 1)
    rel = (ii - jj).astype(F32)
    dmask = jnp.where(rel >= 0.0, jnp.exp(jnp.maximum(rel, 0.0) * log_gamma), 0.0)
    scores = _dot_nt(q.astype(BF16), k.astype(BF16)) * dmask
    o = _dot(scores.astype(BF16), v16)

    idx = lax.broadcasted_iota(jnp.int32, (tb, 1), 0).astype(F32)
    q_in = q * jnp.exp((idx + 1.0) * log_gamma)
    k_tail = k * jnp.exp((tb - 1.0 - idx) * log_gamma)
    state = r_ref[...]
    o = o + _dot(q_in.astype(BF16), state.astype(BF16))
    r_ref[...] = state * jnp.exp(tb * log_gamma) + _dot(k_tail.T.astype(BF16), v16)

    mu = jnp.mean(o, axis=-1, keepdims=True)
    oc = o - mu
    o = oc * lax.rsqrt(jnp.mean(oc * oc, axis=-1, keepdims=True) + LN_EPS)
    o_ref[...] = (o * _silu(g_ref[...])).astype(o_ref.dtype)


def _retention(rest, cos, sin):
    t = rest.shape[0]
    tb = TB_RET
    qk, dv, nh = RET_QK_DIM, RET_V_DIM, RET_HEADS
    return pl.pallas_call(
        _ret_kernel,
        grid=(nh, t // tb),
        in_specs=[pl.BlockSpec((tb, qk), lambda h, i: (i, R_Q // qk + h)),
                  pl.BlockSpec((tb, qk), lambda h, i: (i, R_K // qk + h)),
                  pl.BlockSpec((tb, dv), lambda h, i: (i, R_V // dv + h)),
                  pl.BlockSpec((tb, dv), lambda h, i: (i, R_G // dv + h)),
                  pl.BlockSpec((tb, qk), lambda h, i: (i, 0)),
                  pl.BlockSpec((tb, qk), lambda h, i: (i, 0))],
        out_specs=pl.BlockSpec((tb, dv), lambda h, i: (i, h)),
        out_shape=jax.ShapeDtypeStruct((t, RET_V), BF16),
        scratch_shapes=[pltpu.VMEM((qk, dv), F32)],
        compiler_params=_params("arbitrary", "arbitrary"),
    )(rest, rest, rest, rest, cos, sin)


def _conformer_kernel(ca_ref, cb_ref, w_ref, b_ref, lnw_ref, lnb_ref, o_ref, upad_ref):
    tb = ca_ref.shape[0]

    @pl.when(pl.program_id(0) == 0)
    def _():
        upad_ref[0:HALO, :] = jnp.zeros((HALO, CONV_DIM), F32)

    upad_ref[HALO:, :] = ca_ref[...] * jax.nn.sigmoid(cb_ref[...])
    base = HALO - (CONV_WIDTH - 1)
    acc = w_ref[0:1, :] * upad_ref[base:base + tb, :]
    for j in range(1, CONV_WIDTH):
        acc = acc + w_ref[j:j + 1, :] * upad_ref[base + j:base + j + tb, :]
    upad_ref[0:HALO, :] = upad_ref[tb:tb + HALO, :]
    u = acc + b_ref[...]
    mu = jnp.mean(u, axis=-1, keepdims=True)
    uc = u - mu
    u = uc * lax.rsqrt(jnp.mean(uc * uc, axis=-1, keepdims=True) + LN_EPS) * lnw_ref[...] + lnb_ref[...]
    o_ref[...] = _silu(u).astype(o_ref.dtype)


def _conformer(rest, w, b, ln_w, ln_b):
    t = rest.shape[0]
    tb = TB_CONV
    c = CONV_DIM
    vec = lambda: pl.BlockSpec((1, c), lambda i: (0, 0))
    return pl.pallas_call(
        _conformer_kernel,
        grid=(t // tb,),
        in_specs=[pl.BlockSpec((tb, c), lambda i: (i, R_GLU // c)),
                  pl.BlockSpec((tb, c), lambda i: (i, R_GLU // c + 1)),
                  pl.BlockSpec((CONV_WIDTH, c), lambda i: (0, 0)),
                  vec(), vec(), vec()],
        out_specs=pl.BlockSpec((tb, c), lambda i: (i, 0)),
        out_shape=jax.ShapeDtypeStruct((t, c), BF16),
        scratch_shapes=[pltpu.VMEM((tb + HALO, c), F32)],
        compiler_params=_params("arbitrary"),
    )(rest, rest, w, b.reshape(1, c), ln_w.reshape(1, c), ln_b.reshape(1, c))


def _merge_kernel(oa_ref, ob_ref, oc_ref, wa_ref, wb_ref, wc_ref, ga_ref, gb_ref, gc_ref, o_ref,
                  wa16_ref, wb16_ref, wc16_ref):
    @pl.when(pl.program_id(1) == 0)
    def _():
        wa16_ref[...] = wa_ref[...].astype(BF16)
        wb16_ref[...] = wb_ref[...].astype(BF16)
        wc16_ref[...] = wc_ref[...].astype(BF16)
    m = jax.nn.sigmoid(ga_ref[...]) * _dot(oa_ref[...], wa16_ref[...])
    m = m + jax.nn.sigmoid(gb_ref[...]) * _dot(ob_ref[...], wb16_ref[...])
    m = m + jax.nn.sigmoid(gc_ref[...]) * _dot(oc_ref[...], wc16_ref[...])
    o_ref[...] = m.astype(o_ref.dtype)


def _merge(o_a, o_b, o_c, w_a, w_b, w_c, layer, rest):
    t, k = o_a.shape
    n = w_a.shape[2]
    tn = TN
    act = lambda: pl.BlockSpec((TM, k), lambda j, i: (i, 0))
    wsp = lambda: pl.BlockSpec((None, k, tn), lambda j, i: (layer, 0, j))
    gate = lambda b: pl.BlockSpec((TM, tn), lambda j, i: (i, (R_GATE + b * n) // tn + j))
    return pl.pallas_call(
        _merge_kernel,
        grid=(n // tn, t // TM),
        in_specs=[act(), act(), act(), wsp(), wsp(), wsp(), gate(0), gate(1), gate(2)],
        out_specs=pl.BlockSpec((TM, tn), lambda j, i: (i, j)),
        out_shape=jax.ShapeDtypeStruct((t, n), BF16),
        scratch_shapes=[pltpu.VMEM((k, tn), BF16)] * 3,
        compiler_params=_params("arbitrary", "arbitrary"),
    )(o_a, o_b, o_c, w_a, w_b, w_c, rest, rest, rest)


def kernel(x, c, positions, w_ada, b_ada, norm_mix_w, norm_mlp_w, w_in, conv_qkv_w, gdn_a_log, gdn_dt_bias,
           gdn_norm_w, conv_dw_w, conv_dw_b, conv_ln_w, conv_ln_b, w_branch_a, w_branch_b, w_branch_c,
           w_out, w_mlp_in, w_mlp_out, final_norm_w):
    bsz, t, d = x.shape
    assert bsz == 1
    xs = x.reshape(t, d)
    mod = _adaln_mod(c, w_ada, b_ada)
    cos, sin = _rope_tables(positions.reshape(t))
    for l in range(w_in.shape[0]):
        h = _norm_mod(xs, norm_mix_w[l], mod[l], 0, 1)
        proj_a = _matmul(h, w_in, l, col0=OFF_GDN, ncols=4 * GDN_DIM)
        bg, bgt = _gdn_gates(h, w_in[l, :, OFF_BA:OFF_REST], gdn_a_log[l], gdn_dt_bias[l])
        rest = _matmul(h, w_in[l:l + 1, :, OFF_REST:], 0)
        o_a = _gdn(proj_a, bg, bgt, conv_qkv_w[l], gdn_norm_w[l])
        o_b = _retention(rest, cos, sin)
        o_c = _conformer(rest, conv_dw_w[l], conv_dw_b[l], conv_ln_w[l], conv_ln_b[l])
        merged = _merge(o_a, o_b, o_c, w_branch_a, w_branch_b, w_branch_c, l, rest)
        xs = _matmul_residual(merged, w_out, l, xs, mod[l, 2:3])
        h = _norm_mod(xs, norm_mlp_w[l], mod[l], 3, 4)
        act = _matmul(h, w_mlp_in, l, out_dtype=BF16, relu2=True)
        xs = _mlp_out(act, w_mlp_out, l, xs, mod[l, 5:6])
    return _final_norm(xs, final_norm_w).reshape(bsz, t, d)
```

```python
import functools
import math

import jax
import jax.numpy as jnp
from jax import lax
from jax.experimental import pallas as pl
from jax.experimental.pallas import tpu as pltpu

F32 = jnp.float32
BF16 = jnp.bfloat16

D_MODEL = 2048
GDN_HEADS = 8
GDN_HEAD_DIM = 128
GDN_DIM = GDN_HEADS * GDN_HEAD_DIM
SHORT_CONV = 4
RET_HEADS = 4
RET_QK_DIM = 128
RET_V_DIM = 256
RET_QK = RET_HEADS * RET_QK_DIM
RET_V = RET_HEADS * RET_V_DIM
RET_DECAY_BASE = 5.0
ROPE_BASE = 10000.0
CONV_DIM = D_MODEL // 2
CONV_WIDTH = 31
N_BRANCH = 3
NORM_EPS = 1e-6
LN_EPS = 1e-5
L2_EPS = 1e-6

BA_COLS = 2 * GDN_HEADS
P_GDN = 0
P_RQ = 4 * GDN_DIM
P_RK = P_RQ + RET_QK
P_RV = P_RK + RET_QK
P_RG = P_RV + RET_V
P_GLU = P_RG + RET_V
P_GATE = P_GLU + 2 * CONV_DIM
P_WIDTH = P_GATE + N_BRANCH * D_MODEL

LANES = 128
SUBLANES = 8
VMEM_LIMIT_BYTES = 56 * 1024 * 1024
TM = 1024
TM_MERGE = 512
TN = 1024
GDN_CHUNK = 128
TB_GATE = 512
TB_RET = 256
TB_CONV = 256
TB_NORM = 512
HALO = 32


def _params(*sem):
    return pltpu.CompilerParams(dimension_semantics=sem, vmem_limit_bytes=VMEM_LIMIT_BYTES)


def _dot(a, b):
    return jnp.dot(a, b, preferred_element_type=F32)


def _dot_nt(a, b):
    return lax.dot_general(a, b, (((1,), (1,)), ((), ())), preferred_element_type=F32)


def _dot_exact_lhs(m_bf16, g):
    g1 = g.astype(BF16)
    r1 = g - g1.astype(F32)
    g2 = r1.astype(BF16)
    g3 = (r1 - g2.astype(F32)).astype(BF16)
    return _dot(m_bf16, g1) + _dot(m_bf16, g2) + _dot(m_bf16, g3)


def _silu(x):
    return x * jax.nn.sigmoid(x)


def _mod_kernel(c_ref, w_ref, b_ref, o_ref):
    c = c_ref[...]
    o_ref[0] = jnp.dot(_silu(c), w_ref[0], preferred_element_type=F32,
                       precision=lax.Precision.HIGHEST) + b_ref[0]


def _adaln_mod(c, w_ada, b_ada):
    depth, d, n = w_ada.shape
    tn = 1024
    c8 = jnp.broadcast_to(c, (SUBLANES, d))
    out = pl.pallas_call(
        _mod_kernel,
        grid=(depth, n // tn),
        in_specs=[pl.BlockSpec((SUBLANES, d), lambda l, j: (0, 0)),
                  pl.BlockSpec((1, d, tn), lambda l, j: (l, 0, j)),
                  pl.BlockSpec((1, 1, tn), lambda l, j: (l, 0, j))],
        out_specs=pl.BlockSpec((1, SUBLANES, tn), lambda l, j: (l, 0, j)),
        out_shape=jax.ShapeDtypeStruct((depth, SUBLANES, n), F32),
        compiler_params=_params("arbitrary", "arbitrary"),
    )(c8, w_ada, b_ada.reshape(depth, 1, n))
    return out[:, 0, :].reshape(depth, 6, d)


def _norm_mod_kernel(x_ref, w_ref, mod_ref, o_ref, *, shift_row, scale_row):
    x = x_ref[...]
    y = x * lax.rsqrt(jnp.mean(x * x, axis=-1, keepdims=True) + NORM_EPS) * w_ref[...]
    y = y * (1.0 + mod_ref[scale_row:scale_row + 1, :]) + mod_ref[shift_row:shift_row + 1, :]
    o_ref[...] = y.astype(o_ref.dtype)


def _norm_mod(x, w, mod, shift_row, scale_row):
    t, d = x.shape
    return pl.pallas_call(
        functools.partial(_norm_mod_kernel, shift_row=shift_row, scale_row=scale_row),
        grid=(t // TB_NORM,),
        in_specs=[pl.BlockSpec((TB_NORM, d), lambda i: (i, 0)),
                  pl.BlockSpec((1, d), lambda i: (0, 0)),
                  pl.BlockSpec((6, d), lambda i: (0, 0))],
        out_specs=pl.BlockSpec((TB_NORM, d), lambda i: (i, 0)),
        out_shape=jax.ShapeDtypeStruct((t, d), BF16),
        compiler_params=_params("arbitrary"),
    )(x, w.reshape(1, d), mod)


def _final_norm_kernel(x_ref, w_ref, o_ref):
    x = x_ref[...]
    o_ref[...] = x * lax.rsqrt(jnp.mean(x * x, axis=-1, keepdims=True) + NORM_EPS) * w_ref[...]


def _final_norm(x, w):
    t, d = x.shape
    return pl.pallas_call(
        _final_norm_kernel,
        grid=(t // TB_NORM,),
        in_specs=[pl.BlockSpec((TB_NORM, d), lambda i: (i, 0)),
                  pl.BlockSpec((1, d), lambda i: (0, 0))],
        out_specs=pl.BlockSpec((TB_NORM, d), lambda i: (i, 0)),
        out_shape=jax.ShapeDtypeStruct((t, d), F32),
        compiler_params=_params("arbitrary"),
    )(x, w.reshape(1, d))


def _mm_kernel(a_ref, w_ref, o_ref, wb_ref):
    @pl.when(pl.program_id(1) == 0)
    def _():
        wb_ref[...] = w_ref[...].astype(BF16)
    o_ref[...] = _dot(a_ref[...], wb_ref[...]).astype(o_ref.dtype)


def _mm_relu2_kernel(a_ref, w_ref, o_ref, wb_ref):
    @pl.when(pl.program_id(1) == 0)
    def _():
        wb_ref[...] = w_ref[...].astype(BF16)
    y = jnp.maximum(_dot(a_ref[...], wb_ref[...]), 0.0)
    o_ref[...] = (y * y).astype(o_ref.dtype)


def _mm_residual_kernel(a_ref, w_ref, x_ref, g_ref, o_ref, wb_ref):
    @pl.when(pl.program_id(1) == 0)
    def _():
        wb_ref[...] = w_ref[...].astype(BF16)
    o_ref[...] = x_ref[...] + g_ref[...] * _dot(a_ref[...], wb_ref[...])


def _matmul(a, w, layer, *, out_dtype=F32, relu2=False):
    t, k = a.shape
    n = w.shape[2]
    tn = min(TN, n)
    return pl.pallas_call(
        _mm_relu2_kernel if relu2 else _mm_kernel,
        grid=(n // tn, t // TM),
        in_specs=[pl.BlockSpec((TM, k), lambda j, i: (i, 0)),
                  pl.BlockSpec((None, k, tn), lambda j, i: (layer, 0, j))],
        out_specs=pl.BlockSpec((TM, tn), lambda j, i: (i, j)),
        out_shape=jax.ShapeDtypeStruct((t, n), out_dtype),
        scratch_shapes=[pltpu.VMEM((k, tn), BF16)],
        compiler_params=_params("arbitrary", "arbitrary"),
    )(a, w)


def _matmul_residual(a, w, layer, x, gate):
    t, k = a.shape
    n = w.shape[2]
    tn = min(TN, n)
    return pl.pallas_call(
        _mm_residual_kernel,
        grid=(n // tn, t // TM),
        in_specs=[pl.BlockSpec((TM, k), lambda j, i: (i, 0)),
                  pl.BlockSpec((None, k, tn), lambda j, i: (layer, 0, j)),
                  pl.BlockSpec((TM, tn), lambda j, i: (i, j)),
                  pl.BlockSpec((1, tn), lambda j, i: (0, j))],
        out_specs=pl.BlockSpec((TM, tn), lambda j, i: (i, j)),
        out_shape=jax.ShapeDtypeStruct((t, n), F32),
        scratch_shapes=[pltpu.VMEM((k, tn), BF16)],
        compiler_params=_params("arbitrary", "arbitrary"),
    )(a, w, x, gate)


def _in_proj_kernel(a_ref, w_ref, wn_ref, o_ref, wb_ref, *, plain_tiles):
    j = pl.program_id(0)
    first = pl.program_id(1) == 0

    @pl.when(first & (j < plain_tiles))
    def _():
        wb_ref[...] = w_ref[...].astype(BF16)

    @pl.when(first & (j >= plain_tiles))
    def _():
        keep = LANES - BA_COLS
        lane = lax.broadcasted_iota(jnp.int32, (w_ref.shape[0], LANES), 1)
        nblk = w_ref.shape[1] // LANES
        cur = pltpu.roll(w_ref[:, 0:LANES], keep, axis=1)
        for cblk in range(nblk):
            src = w_ref[:, (cblk + 1) * LANES:(cblk + 2) * LANES] if cblk + 1 < nblk else wn_ref[...]
            nxt = pltpu.roll(src, keep, axis=1)
            wb_ref[:, cblk * LANES:(cblk + 1) * LANES] = jnp.where(lane < keep, cur, nxt).astype(BF16)
            cur = nxt

    o_ref[...] = _dot(a_ref[...], wb_ref[...])


def _in_proj(a, w_in, layer):
    t, k = a.shape
    tn = TN
    per = tn // LANES
    assert P_RQ % tn == 0 and P_WIDTH % tn == 0 and w_in.shape[2] == P_WIDTH + BA_COLS
    return pl.pallas_call(
        functools.partial(_in_proj_kernel, plain_tiles=P_RQ // tn),
        grid=(P_WIDTH // tn, t // TM),
        in_specs=[pl.BlockSpec((TM, k), lambda j, i: (i, 0)),
                  pl.BlockSpec((None, k, tn), lambda j, i: (layer, 0, j)),
                  pl.BlockSpec((None, k, LANES), lambda j, i: (layer, 0, (j + 1) * per))],
        out_specs=pl.BlockSpec((TM, tn), lambda j, i: (i, j)),
        out_shape=jax.ShapeDtypeStruct((t, P_WIDTH), F32),
        scratch_shapes=[pltpu.VMEM((k, tn), BF16)],
        compiler_params=_params("arbitrary", "arbitrary"),
    )(a, w_in, w_in)


def _mlp_out_kernel(a_ref, w_ref, x_ref, g_ref, o_ref, acc_ref):
    kk = pl.program_id(2)

    @pl.when(kk == 0)
    def _():
        acc_ref[...] = jnp.zeros_like(acc_ref)
    acc_ref[...] += _dot(a_ref[...], w_ref[...].astype(BF16))

    @pl.when(kk == pl.num_programs(2) - 1)
    def _():
        o_ref[...] = x_ref[...] + g_ref[...] * acc_ref[...]


def _mlp_out(a, w, layer, x, gate):
    t, k = a.shape
    n = w.shape[2]
    tm, tn, tk = 1024, 1024, 1024
    return pl.pallas_call(
        _mlp_out_kernel,
        grid=(t // tm, n // tn, k // tk),
        in_specs=[pl.BlockSpec((tm, tk), lambda i, j, kk: (i, kk)),
                  pl.BlockSpec((None, tk, tn), lambda i, j, kk: (layer, kk, j)),
                  pl.BlockSpec((tm, tn), lambda i, j, kk: (i, j)),
                  pl.BlockSpec((1, tn), lambda i, j, kk: (0, j))],
        out_specs=pl.BlockSpec((tm, tn), lambda i, j, kk: (i, j)),
        out_shape=jax.ShapeDtypeStruct((t, n), F32),
        scratch_shapes=[pltpu.VMEM((tm, tn), F32)],
        compiler_params=_params("arbitrary", "arbitrary", "arbitrary"),
    )(a, w, x, gate)


def _gdn_gate_kernel(h_ref, w_ref, alog_ref, dtb_ref, bg_ref, bgt_ref):
    tb = h_ref.shape[0]
    ba = _dot(h_ref[...], w_ref[...].astype(BF16))
    beta = jax.nn.sigmoid(ba)
    xs = ba + dtb_ref[...]
    softplus = jnp.maximum(xs, 0.0) + jnp.log1p(jnp.exp(-jnp.abs(xs)))
    g = -jnp.exp(alog_ref[...]) * softplus
    ii = lax.broadcasted_iota(jnp.int32, (tb, tb), 0)
    jj = lax.broadcasted_iota(jnp.int32, (tb, tb), 1)
    shift = int(math.log2(GDN_CHUNK))
    same = jnp.right_shift(ii, shift) == jnp.right_shift(jj, shift)
    tri = jnp.where(same & (ii >= jj), 1.0, 0.0).astype(BF16)
    blk = jnp.where(same, 1.0, 0.0).astype(BF16)
    gc = _dot_exact_lhs(tri, g)
    gl = _dot_exact_lhs(blk, g)
    lane = lax.broadcasted_iota(jnp.int32, (tb, LANES), 1)
    out = jnp.where(lane < GDN_HEADS, beta,
                    jnp.where(lane < 2 * GDN_HEADS, gc,
                              jnp.where(lane < 3 * GDN_HEADS, pltpu.roll(gl, GDN_HEADS, axis=1), 0.0)))
    bg_ref[...] = out
    bgt_ref[...] = out.T


def _gdn_gates(h, w_in, layer, a_log, dt_bias):
    t, d = h.shape
    tb = TB_GATE
    lead = jnp.zeros((GDN_HEADS,), F32)
    tail = jnp.zeros((LANES - 2 * GDN_HEADS,), F32)
    alog = jnp.concatenate([lead, a_log, tail]).reshape(1, LANES)
    dtb = jnp.concatenate([lead, dt_bias, tail]).reshape(1, LANES)
    return pl.pallas_call(
        _gdn_gate_kernel,
        grid=(t // tb,),
        in_specs=[pl.BlockSpec((tb, d), lambda i: (i, 0)),
                  pl.BlockSpec((None, d, LANES), lambda i: (layer, 0, P_RQ // LANES)),
                  pl.BlockSpec((1, LANES), lambda i: (0, 0)),
                  pl.BlockSpec((1, LANES), lambda i: (0, 0))],
        out_specs=[pl.BlockSpec((tb, LANES), lambda i: (i, 0)),
                   pl.BlockSpec((LANES, tb), lambda i: (0, i))],
        out_shape=[jax.ShapeDtypeStruct((t, LANES), F32),
                   jax.ShapeDtypeStruct((LANES, t), F32)],
        compiler_params=_params("arbitrary"),
    )(h, w_in, alog, dtb)


def _tri_inverse_all(lows):
    n = lows[0].shape[0]
    ii = lax.broadcasted_iota(jnp.int32, (n, n), 0)
    jj = lax.broadcasted_iota(jnp.int32, (n, n), 1)
    ts = None
    for level in range(int(math.log2(n))):
        rb = jnp.right_shift(ii, level)
        cb = jnp.right_shift(jj, level)
        sel = (jnp.bitwise_and(rb, 1) == 1) & (cb == rb - 1)
        if level == 0:
            eye = jnp.where(ii == jj, 1.0, 0.0)
            ts = [eye - jnp.where(sel, low, 0.0) for low in lows]
        else:
            t16 = [t.astype(BF16) for t in ts]
            ys = [_dot(jnp.where(sel, low, 0.0).astype(BF16), t).astype(BF16) for low, t in zip(lows, t16)]
            ts = [t - _dot(tb16, y) for t, tb16, y in zip(ts, t16, ys)]
    return ts


def _gdn_kernel(q_ref, k_ref, v_ref, z_ref, bg_ref, bgt_ref, cw_ref, nw_ref, o_ref, xpad_ref, s_ref):
    tb = q_ref.shape[0]
    hd = GDN_HEAD_DIM

    @pl.when(pl.program_id(0) == 0)
    def _():
        xpad_ref[:, 0:SUBLANES, :] = jnp.zeros((3, SUBLANES, GDN_DIM), F32)
        s_ref[...] = jnp.zeros_like(s_ref)

    xpad_ref[0, SUBLANES:, :] = q_ref[...]
    xpad_ref[1, SUBLANES:, :] = k_ref[...]
    xpad_ref[2, SUBLANES:, :] = v_ref[...]

    def conv(i):
        base = SUBLANES - (SHORT_CONV - 1)
        cols = slice(i * GDN_DIM, (i + 1) * GDN_DIM)
        acc = cw_ref[0:1, cols] * xpad_ref[i, base:base + tb, :]
        for j in range(1, SHORT_CONV):
            acc = acc + cw_ref[j:j + 1, cols] * xpad_ref[i, base + j:base + j + tb, :]
        return _silu(acc)

    q_all, k_all, v_all = conv(0), conv(1), conv(2)
    xpad_ref[:, 0:SUBLANES, :] = xpad_ref[:, tb:tb + SUBLANES, :]

    bg = bg_ref[...]
    bgt = bgt_ref[...]
    ii = lax.broadcasted_iota(jnp.int32, (tb, tb), 0)
    jj = lax.broadcasted_iota(jnp.int32, (tb, tb), 1)
    causal = ii >= jj
    strict = ii > jj
    nw = nw_ref[...]
    heads = range(GDN_HEADS)
    sl = [slice(h * hd, (h + 1) * hd) for h in heads]

    l2n = lambda x: x * lax.rsqrt(jnp.sum(x * x, axis=-1, keepdims=True) + L2_EPS)
    q = [l2n(q_all[:, sl[h]]) * (hd ** -0.5) for h in heads]
    k = [l2n(k_all[:, sl[h]]) for h in heads]
    beta = [bg[:, h:h + 1] for h in heads]
    gc = [bg[:, GDN_HEADS + h:GDN_HEADS + h + 1] for h in heads]
    gl = [bg[:, 2 * GDN_HEADS + h:2 * GDN_HEADS + h + 1] for h in heads]
    eg = [jnp.exp(gc[h]) for h in heads]
    decay = [jnp.exp(jnp.where(causal, gc[h] - bgt[h:h + 1, :], -jnp.inf)) for h in heads]
    kb = [k[h] * beta[h] for h in heads]
    kq = [_dot_nt(jnp.concatenate([kb[h], q[h]], axis=0).astype(BF16), k[h].astype(BF16)) for h in heads]
    low = [jnp.where(strict, kq[h][:tb] * decay[h], 0.0) for h in heads]
    attn = [(kq[h][tb:] * decay[h]).astype(BF16) for h in heads]
    t_inv = _tri_inverse_all(low)
    rhs = [jnp.concatenate([v_all[:, sl[h]] * beta[h], kb[h] * eg[h]], axis=1).astype(BF16) for h in heads]
    uw = [_dot(t_inv[h].astype(BF16), rhs[h]) for h in heads]
    state = [s_ref[h] for h in heads]
    ws = [_dot(jnp.concatenate([uw[h][:, hd:], q[h] * eg[h]], axis=0).astype(BF16), state[h].astype(BF16))
          for h in heads]
    v_new = [(uw[h][:, :hd] - ws[h][:tb]).astype(BF16) for h in heads]
    k_dec_t = [(k[h] * jnp.exp(gl[h] - gc[h])).T.astype(BF16) for h in heads]
    for h in heads:
        egl = jnp.broadcast_to(jnp.exp(gl[h]), (tb, hd))[0:1, :]
        s_ref[h] = state[h] * egl + _dot(k_dec_t[h], v_new[h])
    o = [ws[h][tb:] + _dot(attn[h], v_new[h]) for h in heads]
    o = [o[h] * lax.rsqrt(jnp.mean(o[h] * o[h], axis=-1, keepdims=True) + NORM_EPS) * nw for h in heads]
    o_ref[...] = jnp.concatenate([(o[h] * _silu(z_ref[:, sl[h]])).astype(o_ref.dtype) for h in heads], axis=1)


def _gdn(proj, bg, bgt, conv_w, norm_w):
    t = proj.shape[0]
    tb = GDN_CHUNK
    gd, hd = GDN_DIM, GDN_HEAD_DIM
    col = lambda c: pl.BlockSpec((tb, gd), lambda i: (i, P_GDN // gd + c))
    return pl.pallas_call(
        _gdn_kernel,
        grid=(t // tb,),
        in_specs=[col(0), col(1), col(2), col(3),
                  pl.BlockSpec((tb, LANES), lambda i: (i, 0)),
                  pl.BlockSpec((SUBLANES, tb), lambda i: (1, i)),
                  pl.BlockSpec((SHORT_CONV, 3 * gd), lambda i: (0, 0)),
                  pl.BlockSpec((1, hd), lambda i: (0, 0))],
        out_specs=pl.BlockSpec((tb, gd), lambda i: (i, 0)),
        out_shape=jax.ShapeDtypeStruct((t, gd), BF16),
        scratch_shapes=[pltpu.VMEM((3, tb + SUBLANES, gd), F32),
                        pltpu.VMEM((GDN_HEADS, hd, hd), F32)],
        compiler_params=_params("arbitrary"),
    )(proj, proj, proj, proj, bg, bgt, conv_w, norm_w.reshape(1, hd))


def _rope_kernel(pos_ref, cos_ref, sin_ref):
    half = RET_QK_DIM // 2
    lane = lax.broadcasted_iota(jnp.int32, (1, RET_QK_DIM), 1)
    idx = jnp.where(lane < half, lane, lane - half).astype(F32)
    inv_freq = jnp.exp(idx * (-math.log(ROPE_BASE) / half))
    ang = pos_ref[...].astype(F32) * inv_freq
    cos_ref[...] = jnp.cos(ang)
    sin_ref[...] = jnp.where(lane < half, -1.0, 1.0) * jnp.sin(ang)


def _rope_tables(positions):
    t = positions.shape[0]
    tb = 1024
    return pl.pallas_call(
        _rope_kernel,
        grid=(t // tb,),
        in_specs=[pl.BlockSpec((tb, 1), lambda i: (i, 0))],
        out_specs=[pl.BlockSpec((tb, RET_QK_DIM), lambda i: (i, 0))] * 2,
        out_shape=[jax.ShapeDtypeStruct((t, RET_QK_DIM), F32)] * 2,
        compiler_params=_params("arbitrary"),
    )(positions.reshape(t, 1))


def _ret_kernel(q_ref, k_ref, v_ref, g_ref, cos_ref, sin_ref, o_ref, r_ref):
    tb = q_ref.shape[0]
    head = pl.program_id(0)

    @pl.when(pl.program_id(1) == 0)
    def _():
        r_ref[...] = jnp.zeros_like(r_ref)

    hf = jnp.full((1, 1), head, jnp.int32).astype(F32)
    log_gamma = jnp.log(1.0 - jnp.exp2(-RET_DECAY_BASE - hf))
    cos, sin = cos_ref[...], sin_ref[...]
    half = RET_QK_DIM // 2
    rope = lambda x: x * cos + pltpu.roll(x, half, axis=1) * sin
    q = rope(q_ref[...])
    k = rope(k_ref[...]) * (RET_QK_DIM ** -0.5)
    v16 = v_ref[...].astype(BF16)

    ii = lax.broadcasted_iota(jnp.int32, (tb, tb), 0)
    jj = lax.broadcasted_iota(jnp.int32, (tb, tb), 1)
    rel = (ii - jj).astype(F32)
    dmask = jnp.where(rel >= 0.0, jnp.exp(jnp.maximum(rel, 0.0) * log_gamma), 0.0)
    scores = _dot_nt(q.astype(BF16), k.astype(BF16)) * dmask
    o = _dot(scores.astype(BF16), v16)

    idx = lax.broadcasted_iota(jnp.int32, (tb, 1), 0).astype(F32)
    q_in = q * jnp.exp((idx + 1.0) * log_gamma)
    k_tail = k * jnp.exp((tb - 1.0 - idx) * log_gamma)
    state = r_ref[...]
    o = o + _dot(q_in.astype(BF16), state.astype(BF16))
    r_ref[...] = state * jnp.exp(tb * log_gamma) + _dot(k_tail.T.astype(BF16), v16)

    mu = jnp.mean(o, axis=-1, keepdims=True)
    oc = o - mu
    o = oc * lax.rsqrt(jnp.mean(oc * oc, axis=-1, keepdims=True) + LN_EPS)
    o_ref[...] = (o * _silu(g_ref[...])).astype(o_ref.dtype)


def _retention(proj, cos, sin):
    t = proj.shape[0]
    tb = TB_RET
    qk, dv, nh = RET_QK_DIM, RET_V_DIM, RET_HEADS
    return pl.pallas_call(
        _ret_kernel,
        grid=(nh, t // tb),
        in_specs=[pl.BlockSpec((tb, qk), lambda h, i: (i, P_RQ // qk + h)),
                  pl.BlockSpec((tb, qk), lambda h, i: (i, P_RK // qk + h)),
                  pl.BlockSpec((tb, dv), lambda h, i: (i, P_RV // dv + h)),
                  pl.BlockSpec((tb, dv), lambda h, i: (i, P_RG // dv + h)),
                  pl.BlockSpec((tb, qk), lambda h, i: (i, 0)),
                  pl.BlockSpec((tb, qk), lambda h, i: (i, 0))],
        out_specs=pl.BlockSpec((tb, dv), lambda h, i: (i, h)),
        out_shape=jax.ShapeDtypeStruct((t, RET_V), BF16),
        scratch_shapes=[pltpu.VMEM((qk, dv), F32)],
        compiler_params=_params("arbitrary", "arbitrary"),
    )(proj, proj, proj, proj, cos, sin)


def _conformer_kernel(ca_ref, cb_ref, w_ref, b_ref, lnw_ref, lnb_ref, o_ref, upad_ref):
    tb = ca_ref.shape[0]

    @pl.when(pl.program_id(0) == 0)
    def _():
        upad_ref[0:HALO, :] = jnp.zeros((HALO, CONV_DIM), F32)

    upad_ref[HALO:, :] = ca_ref[...] * jax.nn.sigmoid(cb_ref[...])
    base = HALO - (CONV_WIDTH - 1)
    acc = w_ref[0:1, :] * upad_ref[base:base + tb, :]
    for j in range(1, CONV_WIDTH):
        acc = acc + w_ref[j:j + 1, :] * upad_ref[base + j:base + j + tb, :]
    upad_ref[0:HALO, :] = upad_ref[tb:tb + HALO, :]
    u = acc + b_ref[...]
    mu = jnp.mean(u, axis=-1, keepdims=True)
    uc = u - mu
    u = uc * lax.rsqrt(jnp.mean(uc * uc, axis=-1, keepdims=True) + LN_EPS) * lnw_ref[...] + lnb_ref[...]
    o_ref[...] = _silu(u).astype(o_ref.dtype)


def _conformer(proj, w, b, ln_w, ln_b):
    t = proj.shape[0]
    tb = TB_CONV
    c = CONV_DIM
    vec = lambda: pl.BlockSpec((1, c), lambda i: (0, 0))
    return pl.pallas_call(
        _conformer_kernel,
        grid=(t // tb,),
        in_specs=[pl.BlockSpec((tb, c), lambda i: (i, P_GLU // c)),
                  pl.BlockSpec((tb, c), lambda i: (i, P_GLU // c + 1)),
                  pl.BlockSpec((CONV_WIDTH, c), lambda i: (0, 0)),
                  vec(), vec(), vec()],
        out_specs=pl.BlockSpec((tb, c), lambda i: (i, 0)),
        out_shape=jax.ShapeDtypeStruct((t, c), BF16),
        scratch_shapes=[pltpu.VMEM((tb + HALO, c), F32)],
        compiler_params=_params("arbitrary"),
    )(proj, proj, w, b.reshape(1, c), ln_w.reshape(1, c), ln_b.reshape(1, c))


def _merge_kernel(oa_ref, ob_ref, oc_ref, wa_ref, wb_ref, wc_ref, ga_ref, gb_ref, gc_ref, o_ref,
                  wa16_ref, wb16_ref, wc16_ref):
    @pl.when(pl.program_id(1) == 0)
    def _():
        wa16_ref[...] = wa_ref[...].astype(BF16)
        wb16_ref[...] = wb_ref[...].astype(BF16)
        wc16_ref[...] = wc_ref[...].astype(BF16)
    m = jax.nn.sigmoid(ga_ref[...]) * _dot(oa_ref[...], wa16_ref[...])
    m = m + jax.nn.sigmoid(gb_ref[...]) * _dot(ob_ref[...], wb16_ref[...])
    m = m + jax.nn.sigmoid(gc_ref[...]) * _dot(oc_ref[...], wc16_ref[...])
    o_ref[...] = m.astype(o_ref.dtype)


def _merge(o_a, o_b, o_c, w_a, w_b, w_c, layer, proj):
    t, k = o_a.shape
    n = w_a.shape[2]
    tm, tn = TM_MERGE, TN
    act = lambda: pl.BlockSpec((tm, k), lambda j, i: (i, 0))
    wsp = lambda: pl.BlockSpec((None, k, tn), lambda j, i: (layer, 0, j))
    gate = lambda b: pl.BlockSpec((tm, tn), lambda j, i: (i, (P_GATE + b * n) // tn + j))
    return pl.pallas_call(
        _merge_kernel,
        grid=(n // tn, t // tm),
        in_specs=[act(), act(), act(), wsp(), wsp(), wsp(), gate(0), gate(1), gate(2)],
        out_specs=pl.BlockSpec((tm, tn), lambda j, i: (i, j)),
        out_shape=jax.ShapeDtypeStruct((t, n), BF16),
        scratch_shapes=[pltpu.VMEM((k, tn), BF16)] * 3,
        compiler_params=_params("arbitrary", "arbitrary"),
    )(o_a, o_b, o_c, w_a, w_b, w_c, proj, proj, proj)


def kernel(x, c, positions, w_ada, b_ada, norm_mix_w, norm_mlp_w, w_in, conv_qkv_w, gdn_a_log, gdn_dt_bias,
           gdn_norm_w, conv_dw_w, conv_dw_b, conv_ln_w, conv_ln_b, w_branch_a, w_branch_b, w_branch_c,
           w_out, w_mlp_in, w_mlp_out, final_norm_w):
    bsz, t, d = x.shape
    assert bsz == 1
    xs = x.reshape(t, d)
    mod = _adaln_mod(c, w_ada, b_ada)
    cos, sin = _rope_tables(positions.reshape(t))
    for l in range(w_in.shape[0]):
        h = _norm_mod(xs, norm_mix_w[l], mod[l], 0, 1)
        proj = _in_proj(h, w_in, l)
        bg, bgt = _gdn_gates(h, w_in, l, gdn_a_log[l], gdn_dt_bias[l])
        o_a = _gdn(proj, bg, bgt, conv_qkv_w[l], gdn_norm_w[l])
        o_b = _retention(proj, cos, sin)
        o_c = _conformer(proj, conv_dw_w[l], conv_dw_b[l], conv_ln_w[l], conv_ln_b[l])
        merged = _merge(o_a, o_b, o_c, w_branch_a, w_branch_b, w_branch_c, l, proj)
        xs = _matmul_residual(merged, w_out, l, xs, mod[l, 2:3])
        h = _norm_mod(xs, norm_mlp_w[l], mod[l], 3, 4)
        act = _matmul(h, w_mlp_in, l, out_dtype=BF16, relu2=True)
        xs = _mlp_out(act, w_mlp_out, l, xs, mod[l, 5:6])
    return _final_norm(xs, final_norm_w).reshape(bsz, t, d)
```

```python
import functools
import math

import jax
import jax.numpy as jnp
from jax import lax
from jax.experimental import pallas as pl
from jax.experimental.pallas import tpu as pltpu

F32 = jnp.float32
BF16 = jnp.bfloat16

D_MODEL = 2048
GDN_HEADS = 8
GDN_HEAD_DIM = 128
GDN_DIM = GDN_HEADS * GDN_HEAD_DIM
SHORT_CONV = 4
RET_HEADS = 4
RET_QK_DIM = 128
RET_V_DIM = 256
RET_QK = RET_HEADS * RET_QK_DIM
RET_V = RET_HEADS * RET_V_DIM
RET_DECAY_BASE = 5.0
ROPE_BASE = 10000.0
CONV_DIM = D_MODEL // 2
CONV_WIDTH = 31
N_BRANCH = 3
NORM_EPS = 1e-6
LN_EPS = 1e-5
L2_EPS = 1e-6

BA_COLS = 2 * GDN_HEADS
P_GDN = 0
P_RQ = 4 * GDN_DIM
P_RK = P_RQ + RET_QK
P_RV = P_RK + RET_QK
P_RG = P_RV + RET_V
P_GLU = P_RG + RET_V
P_GATE = P_GLU + 2 * CONV_DIM
P_WIDTH = P_GATE + N_BRANCH * D_MODEL

LANES = 128
SUBLANES = 8
VMEM_LIMIT_BYTES = 56 * 1024 * 1024
TM = 1024
TM_MERGE = 512
TN = 1024
GDN_CHUNK = 128
TB_GATE = 512
TB_RET = 256
TB_CONV = 256
TB_NORM = 512
HALO = 32


def _params(*sem):
    return pltpu.CompilerParams(dimension_semantics=sem, vmem_limit_bytes=VMEM_LIMIT_BYTES)


def _dot(a, b):
    return jnp.dot(a, b, preferred_element_type=F32)


def _dot_nt(a, b):
    return lax.dot_general(a, b, (((1,), (1,)), ((), ())), preferred_element_type=F32)


def _dot_exact_lhs(m_bf16, g):
    g1 = g.astype(BF16)
    r1 = g - g1.astype(F32)
    g2 = r1.astype(BF16)
    g3 = (r1 - g2.astype(F32)).astype(BF16)
    return _dot(m_bf16, g1) + _dot(m_bf16, g2) + _dot(m_bf16, g3)


def _silu(x):
    return x * jax.nn.sigmoid(x)


def _mod_kernel(c_ref, w_ref, b_ref, o_ref):
    c = c_ref[...]
    o_ref[0] = jnp.dot(_silu(c), w_ref[0], preferred_element_type=F32,
                       precision=lax.Precision.HIGHEST) + b_ref[0]


def _adaln_mod(c, w_ada, b_ada):
    depth, d, n = w_ada.shape
    tn = 1024
    c8 = jnp.broadcast_to(c, (SUBLANES, d))
    out = pl.pallas_call(
        _mod_kernel,
        grid=(depth, n // tn),
        in_specs=[pl.BlockSpec((SUBLANES, d), lambda l, j: (0, 0)),
                  pl.BlockSpec((1, d, tn), lambda l, j: (l, 0, j)),
                  pl.BlockSpec((1, 1, tn), lambda l, j: (l, 0, j))],
        out_specs=pl.BlockSpec((1, SUBLANES, tn), lambda l, j: (l, 0, j)),
        out_shape=jax.ShapeDtypeStruct((depth, SUBLANES, n), F32),
        compiler_params=_params("arbitrary", "arbitrary"),
    )(c8, w_ada, b_ada.reshape(depth, 1, n))
    return out[:, 0, :].reshape(depth, 6, d)


def _norm_mod_kernel(x_ref, w_ref, mod_ref, o_ref, *, shift_row, scale_row):
    x = x_ref[...]
    y = x * lax.rsqrt(jnp.mean(x * x, axis=-1, keepdims=True) + NORM_EPS) * w_ref[...]
    y = y * (1.0 + mod_ref[scale_row:scale_row + 1, :]) + mod_ref[shift_row:shift_row + 1, :]
    o_ref[...] = y.astype(o_ref.dtype)


def _norm_mod(x, w, mod, shift_row, scale_row):
    t, d = x.shape
    return pl.pallas_call(
        functools.partial(_norm_mod_kernel, shift_row=shift_row, scale_row=scale_row),
        grid=(t // TB_NORM,),
        in_specs=[pl.BlockSpec((TB_NORM, d), lambda i: (i, 0)),
                  pl.BlockSpec((1, d), lambda i: (0, 0)),
                  pl.BlockSpec((6, d), lambda i: (0, 0))],
        out_specs=pl.BlockSpec((TB_NORM, d), lambda i: (i, 0)),
        out_shape=jax.ShapeDtypeStruct((t, d), BF16),
        compiler_params=_params("arbitrary"),
    )(x, w.reshape(1, d), mod)


def _final_norm_kernel(x_ref, w_ref, o_ref):
    x = x_ref[...]
    o_ref[...] = x * lax.rsqrt(jnp.mean(x * x, axis=-1, keepdims=True) + NORM_EPS) * w_ref[...]


def _final_norm(x, w):
    t, d = x.shape
    return pl.pallas_call(
        _final_norm_kernel,
        grid=(t // TB_NORM,),
        in_specs=[pl.BlockSpec((TB_NORM, d), lambda i: (i, 0)),
                  pl.BlockSpec((1, d), lambda i: (0, 0))],
        out_specs=pl.BlockSpec((TB_NORM, d), lambda i: (i, 0)),
        out_shape=jax.ShapeDtypeStruct((t, d), F32),
        compiler_params=_params("arbitrary"),
    )(x, w.reshape(1, d))


def _mm_kernel(a_ref, w_ref, o_ref, wb_ref):
    @pl.when(pl.program_id(1) == 0)
    def _():
        wb_ref[...] = w_ref[...].astype(BF16)
    o_ref[...] = _dot(a_ref[...], wb_ref[...]).astype(o_ref.dtype)


def _mm_relu2_kernel(a_ref, w_ref, o_ref, wb_ref):
    @pl.when(pl.program_id(1) == 0)
    def _():
        wb_ref[...] = w_ref[...].astype(BF16)
    y = jnp.maximum(_dot(a_ref[...], wb_ref[...]), 0.0)
    o_ref[...] = (y * y).astype(o_ref.dtype)


def _mm_residual_kernel(a_ref, w_ref, x_ref, g_ref, o_ref, wb_ref):
    @pl.when(pl.program_id(1) == 0)
    def _():
        wb_ref[...] = w_ref[...].astype(BF16)
    o_ref[...] = x_ref[...] + g_ref[...] * _dot(a_ref[...], wb_ref[...])


def _matmul(a, w, layer, *, out_dtype=F32, relu2=False):
    t, k = a.shape
    n = w.shape[2]
    tn = min(TN, n)
    return pl.pallas_call(
        _mm_relu2_kernel if relu2 else _mm_kernel,
        grid=(n // tn, t // TM),
        in_specs=[pl.BlockSpec((TM, k), lambda j, i: (i, 0)),
                  pl.BlockSpec((None, k, tn), lambda j, i: (layer, 0, j))],
        out_specs=pl.BlockSpec((TM, tn), lambda j, i: (i, j)),
        out_shape=jax.ShapeDtypeStruct((t, n), out_dtype),
        scratch_shapes=[pltpu.VMEM((k, tn), BF16)],
        compiler_params=_params("arbitrary", "arbitrary"),
    )(a, w)


def _matmul_residual(a, w, layer, x, gate):
    t, k = a.shape
    n = w.shape[2]
    tn = min(TN, n)
    return pl.pallas_call(
        _mm_residual_kernel,
        grid=(n // tn, t // TM),
        in_specs=[pl.BlockSpec((TM, k), lambda j, i: (i, 0)),
                  pl.BlockSpec((None, k, tn), lambda j, i: (layer, 0, j)),
                  pl.BlockSpec((TM, tn), lambda j, i: (i, j)),
                  pl.BlockSpec((1, tn), lambda j, i: (0, j))],
        out_specs=pl.BlockSpec((TM, tn), lambda j, i: (i, j)),
        out_shape=jax.ShapeDtypeStruct((t, n), F32),
        scratch_shapes=[pltpu.VMEM((k, tn), BF16)],
        compiler_params=_params("arbitrary", "arbitrary"),
    )(a, w, x, gate)


def _in_proj_kernel(a_ref, w_ref, wn_ref, o_ref, wb_ref, *, plain_tiles):
    j = pl.program_id(0)
    first = pl.program_id(1) == 0

    @pl.when(first & (j < plain_tiles))
    def _():
        wb_ref[...] = w_ref[...].T.astype(BF16)

    @pl.when(first & (j >= plain_tiles))
    def _():
        wb_ref[...] = jnp.concatenate([w_ref[BA_COLS:, :], wn_ref[...]], axis=0).T.astype(BF16)

    o_ref[...] = _dot(a_ref[...], wb_ref[...])


def _in_proj(a, w_in_t, layer):
    t, k = a.shape
    tn = TN
    assert P_RQ % tn == 0 and P_WIDTH % tn == 0 and w_in_t.shape[1] == P_WIDTH + BA_COLS
    return pl.pallas_call(
        functools.partial(_in_proj_kernel, plain_tiles=P_RQ // tn),
        grid=(P_WIDTH // tn, t // TM),
        in_specs=[pl.BlockSpec((TM, k), lambda j, i: (i, 0)),
                  pl.BlockSpec((None, tn, k), lambda j, i: (layer, j, 0)),
                  pl.BlockSpec((None, BA_COLS, k), lambda j, i: (layer, (j + 1) * (tn // BA_COLS), 0))],
        out_specs=pl.BlockSpec((TM, tn), lambda j, i: (i, j)),
        out_shape=jax.ShapeDtypeStruct((t, P_WIDTH), F32),
        scratch_shapes=[pltpu.VMEM((k, tn), BF16)],
        compiler_params=_params("arbitrary", "arbitrary"),
    )(a, w_in_t, w_in_t)


def _mlp_out_kernel(a_ref, w_ref, x_ref, g_ref, o_ref, acc_ref):
    kk = pl.program_id(2)

    @pl.when(kk == 0)
    def _():
        acc_ref[...] = jnp.zeros_like(acc_ref)
    acc_ref[...] += _dot(a_ref[...], w_ref[...].astype(BF16))

    @pl.when(kk == pl.num_programs(2) - 1)
    def _():
        o_ref[...] = x_ref[...] + g_ref[...] * acc_ref[...]


def _mlp_out(a, w, layer, x, gate):
    t, k = a.shape
    n = w.shape[2]
    tm, tn, tk = 1024, 1024, 1024
    return pl.pallas_call(
        _mlp_out_kernel,
        grid=(t // tm, n // tn, k // tk),
        in_specs=[pl.BlockSpec((tm, tk), lambda i, j, kk: (i, kk)),
                  pl.BlockSpec((None, tk, tn), lambda i, j, kk: (layer, kk, j)),
                  pl.BlockSpec((tm, tn), lambda i, j, kk: (i, j)),
                  pl.BlockSpec((1, tn), lambda i, j, kk: (0, j))],
        out_specs=pl.BlockSpec((tm, tn), lambda i, j, kk: (i, j)),
        out_shape=jax.ShapeDtypeStruct((t, n), F32),
        scratch_shapes=[pltpu.VMEM((tm, tn), F32)],
        compiler_params=_params("arbitrary", "arbitrary", "arbitrary"),
    )(a, w, x, gate)


def _gdn_gate_kernel(h_ref, w_ref, alog_ref, dtb_ref, bg_ref, bgt_ref):
    tb = h_ref.shape[0]
    ba = _dot_nt(h_ref[...], w_ref[...].astype(BF16))
    beta = jax.nn.sigmoid(ba)
    xs = ba + dtb_ref[...]
    softplus = jnp.maximum(xs, 0.0) + jnp.log1p(jnp.exp(-jnp.abs(xs)))
    g = -jnp.exp(alog_ref[...]) * softplus
    ii = lax.broadcasted_iota(jnp.int32, (tb, tb), 0)
    jj = lax.broadcasted_iota(jnp.int32, (tb, tb), 1)
    shift = int(math.log2(GDN_CHUNK))
    same = jnp.right_shift(ii, shift) == jnp.right_shift(jj, shift)
    tri = jnp.where(same & (ii >= jj), 1.0, 0.0).astype(BF16)
    blk = jnp.where(same, 1.0, 0.0).astype(BF16)
    gc = _dot_exact_lhs(tri, g)
    gl = _dot_exact_lhs(blk, g)
    lane = lax.broadcasted_iota(jnp.int32, (tb, LANES), 1)
    out = jnp.where(lane < GDN_HEADS, beta,
                    jnp.where(lane < 2 * GDN_HEADS, gc,
                              jnp.where(lane < 3 * GDN_HEADS, pltpu.roll(gl, GDN_HEADS, axis=1), 0.0)))
    bg_ref[...] = out
    bgt_ref[...] = out.T


def _gdn_gates(h, w_in_t, layer, a_log, dt_bias):
    t, d = h.shape
    tb = TB_GATE
    lead = jnp.zeros((GDN_HEADS,), F32)
    tail = jnp.zeros((LANES - 2 * GDN_HEADS,), F32)
    alog = jnp.concatenate([lead, a_log, tail]).reshape(1, LANES)
    dtb = jnp.concatenate([lead, dt_bias, tail]).reshape(1, LANES)
    return pl.pallas_call(
        _gdn_gate_kernel,
        grid=(t // tb,),
        in_specs=[pl.BlockSpec((tb, d), lambda i: (i, 0)),
                  pl.BlockSpec((None, LANES, d), lambda i: (layer, P_RQ // LANES, 0)),
                  pl.BlockSpec((1, LANES), lambda i: (0, 0)),
                  pl.BlockSpec((1, LANES), lambda i: (0, 0))],
        out_specs=[pl.BlockSpec((tb, LANES), lambda i: (i, 0)),
                   pl.BlockSpec((LANES, tb), lambda i: (0, i))],
        out_shape=[jax.ShapeDtypeStruct((t, LANES), F32),
                   jax.ShapeDtypeStruct((LANES, t), F32)],
        compiler_params=_params("arbitrary"),
    )(h, w_in_t, alog, dtb)


def _tri_inverse_all(lows):
    n = lows[0].shape[0]
    ii = lax.broadcasted_iota(jnp.int32, (n, n), 0)
    jj = lax.broadcasted_iota(jnp.int32, (n, n), 1)
    ts = None
    for level in range(int(math.log2(n))):
        rb = jnp.right_shift(ii, level)
        cb = jnp.right_shift(jj, level)
        sel = (jnp.bitwise_and(rb, 1) == 1) & (cb == rb - 1)
        if level == 0:
            eye = jnp.where(ii == jj, 1.0, 0.0)
            ts = [eye - jnp.where(sel, low, 0.0) for low in lows]
        else:
            t16 = [t.astype(BF16) for t in ts]
            ys = [_dot(jnp.where(sel, low, 0.0).astype(BF16), t).astype(BF16) for low, t in zip(lows, t16)]
            ts = [t - _dot(tb16, y) for t, tb16, y in zip(ts, t16, ys)]
    return ts


def _gdn_kernel(q_ref, k_ref, v_ref, z_ref, bg_ref, bgt_ref, cw_ref, nw_ref, o_ref, xpad_ref, s_ref):
    tb = q_ref.shape[0]
    hd = GDN_HEAD_DIM

    @pl.when(pl.program_id(0) == 0)
    def _():
        xpad_ref[:, 0:SUBLANES, :] = jnp.zeros((3, SUBLANES, GDN_DIM), F32)
        s_ref[...] = jnp.zeros_like(s_ref)

    xpad_ref[0, SUBLANES:, :] = q_ref[...]
    xpad_ref[1, SUBLANES:, :] = k_ref[...]
    xpad_ref[2, SUBLANES:, :] = v_ref[...]

    def conv(i):
        cols = slice(i * GDN_DIM, (i + 1) * GDN_DIM)
        full = xpad_ref[i]
        acc = cw_ref[SHORT_CONV - 1:SHORT_CONV, cols] * full[SUBLANES:]
        for j in range(SHORT_CONV - 1):
            shifted = pltpu.roll(full, SHORT_CONV - 1 - j, axis=0)
            acc = acc + cw_ref[j:j + 1, cols] * shifted[SUBLANES:]
        return _silu(acc)

    q_all, k_all, v_all = conv(0), conv(1), conv(2)
    xpad_ref[:, 0:SUBLANES, :] = xpad_ref[:, tb:tb + SUBLANES, :]

    bg = bg_ref[...]
    bgt = bgt_ref[...]
    ii = lax.broadcasted_iota(jnp.int32, (tb, tb), 0)
    jj = lax.broadcasted_iota(jnp.int32, (tb, tb), 1)
    causal = ii >= jj
    strict = ii > jj
    nw = nw_ref[...]
    heads = range(GDN_HEADS)
    sl = [slice(h * hd, (h + 1) * hd) for h in heads]

    l2n = lambda x: x * lax.rsqrt(jnp.sum(x * x, axis=-1, keepdims=True) + L2_EPS)
    q = [l2n(q_all[:, sl[h]]) * (hd ** -0.5) for h in heads]
    k = [l2n(k_all[:, sl[h]]) for h in heads]
    beta = [bg[:, h:h + 1] for h in heads]
    gc = [bg[:, GDN_HEADS + h:GDN_HEADS + h + 1] for h in heads]
    gl = [bg[:, 2 * GDN_HEADS + h:2 * GDN_HEADS + h + 1] for h in heads]
    eg = [jnp.exp(gc[h]) for h in heads]
    decay = [jnp.exp(jnp.where(causal, gc[h] - bgt[h:h + 1, :], -jnp.inf)) for h in heads]
    kb = [k[h] * beta[h] for h in heads]
    kq = [_dot_nt(jnp.concatenate([kb[h], q[h]], axis=0).astype(BF16), k[h].astype(BF16)) for h in heads]
    low = [jnp.where(strict, kq[h][:tb] * decay[h], 0.0) for h in heads]
    attn = [(kq[h][tb:] * decay[h]).astype(BF16) for h in heads]
    t_inv = _tri_inverse_all(low)
    rhs = [jnp.concatenate([v_all[:, sl[h]] * beta[h], kb[h] * eg[h]], axis=1).astype(BF16) for h in heads]
    uw = [_dot(t_inv[h].astype(BF16), rhs[h]) for h in heads]
    state = [s_ref[h] for h in heads]
    ws = [_dot(jnp.concatenate([uw[h][:, hd:], q[h] * eg[h]], axis=0).astype(BF16), state[h].astype(BF16))
          for h in heads]
    v_new = [(uw[h][:, :hd] - ws[h][:tb]).astype(BF16) for h in heads]
    k_dec_t = [(k[h] * jnp.exp(gl[h] - gc[h])).T.astype(BF16) for h in heads]
    for h in heads:
        egl = jnp.broadcast_to(jnp.exp(gl[h]), (tb, hd))[0:1, :]
        s_ref[h] = state[h] * egl + _dot(k_dec_t[h], v_new[h])
    o = [ws[h][tb:] + _dot(attn[h], v_new[h]) for h in heads]
    o = [o[h] * lax.rsqrt(jnp.mean(o[h] * o[h], axis=-1, keepdims=True) + NORM_EPS) * nw for h in heads]
    o_ref[...] = jnp.concatenate([(o[h] * _silu(z_ref[:, sl[h]])).astype(o_ref.dtype) for h in heads], axis=1)


def _gdn(proj, bg, bgt, conv_w, norm_w):
    t = proj.shape[0]
    tb = GDN_CHUNK
    gd, hd = GDN_DIM, GDN_HEAD_DIM
    col = lambda c: pl.BlockSpec((tb, gd), lambda i: (i, P_GDN // gd + c))
    return pl.pallas_call(
        _gdn_kernel,
        grid=(t // tb,),
        in_specs=[col(0), col(1), col(2), col(3),
                  pl.BlockSpec((tb, LANES), lambda i: (i, 0)),
                  pl.BlockSpec((SUBLANES, tb), lambda i: (1, i)),
                  pl.BlockSpec((SHORT_CONV, 3 * gd), lambda i: (0, 0)),
                  pl.BlockSpec((1, hd), lambda i: (0, 0))],
        out_specs=pl.BlockSpec((tb, gd), lambda i: (i, 0)),
        out_shape=jax.ShapeDtypeStruct((t, gd), BF16),
        scratch_shapes=[pltpu.VMEM((3, tb + SUBLANES, gd), F32),
                        pltpu.VMEM((GDN_HEADS, hd, hd), F32)],
        compiler_params=_params("arbitrary"),
    )(proj, proj, proj, proj, bg, bgt, conv_w, norm_w.reshape(1, hd))


def _rope_kernel(pos_ref, cos_ref, sin_ref):
    half = RET_QK_DIM // 2
    lane = lax.broadcasted_iota(jnp.int32, (1, RET_QK_DIM), 1)
    idx = jnp.where(lane < half, lane, lane - half).astype(F32)
    inv_freq = jnp.exp(idx * (-math.log(ROPE_BASE) / half))
    ang = pos_ref[...].astype(F32) * inv_freq
    cos_ref[...] = jnp.cos(ang)
    sin_ref[...] = jnp.where(lane < half, -1.0, 1.0) * jnp.sin(ang)


def _rope_tables(positions):
    t = positions.shape[0]
    tb = 1024
    return pl.pallas_call(
        _rope_kernel,
        grid=(t // tb,),
        in_specs=[pl.BlockSpec((tb, 1), lambda i: (i, 0))],
        out_specs=[pl.BlockSpec((tb, RET_QK_DIM), lambda i: (i, 0))] * 2,
        out_shape=[jax.ShapeDtypeStruct((t, RET_QK_DIM), F32)] * 2,
        compiler_params=_params("arbitrary"),
    )(positions.reshape(t, 1))


def _ret_kernel(q_ref, k_ref, v_ref, g_ref, cos_ref, sin_ref, o_ref, r_ref):
    tb = q_ref.shape[0]
    head = pl.program_id(0)

    @pl.when(pl.program_id(1) == 0)
    def _():
        r_ref[...] = jnp.zeros_like(r_ref)

    hf = jnp.full((1, 1), head, jnp.int32).astype(F32)
    log_gamma = jnp.log(1.0 - jnp.exp2(-RET_DECAY_BASE - hf))
    cos, sin = cos_ref[...], sin_ref[...]
    half = RET_QK_DIM // 2
    rope = lambda x: x * cos + pltpu.roll(x, half, axis=1) * sin
    q = rope(q_ref[...])
    k = rope(k_ref[...]) * (RET_QK_DIM ** -0.5)
    v16 = v_ref[...].astype(BF16)

    ii = lax.broadcasted_iota(jnp.int32, (tb, tb), 0)
    jj = lax.broadcasted_iota(jnp.int32, (tb, tb), 1)
    rel = (ii - jj).astype(F32)
    dmask = jnp.where(rel >= 0.0, jnp.exp(jnp.maximum(rel, 0.0) * log_gamma), 0.0)
    scores = _dot_nt(q.astype(BF16), k.astype(BF16)) * dmask
    o = _dot(scores.astype(BF16), v16)

    idx = lax.broadcasted_iota(jnp.int32, (tb, 1), 0).astype(F32)
    q_in = q * jnp.exp((idx + 1.0) * log_gamma)
    k_tail = k * jnp.exp((tb - 1.0 - idx) * log_gamma)
    state = r_ref[...]
    o = o + _dot(q_in.astype(BF16), state.astype(BF16))
    r_ref[...] = state * jnp.exp(tb * log_gamma) + _dot(k_tail.T.astype(BF16), v16)

    mu = jnp.mean(o, axis=-1, keepdims=True)
    oc = o - mu
    o = oc * lax.rsqrt(jnp.mean(oc * oc, axis=-1, keepdims=True) + LN_EPS)
    o_ref[...] = (o * _silu(g_ref[...])).astype(o_ref.dtype)


def _retention(proj, cos, sin):
    t = proj.shape[0]
    tb = TB_RET
    qk, dv, nh = RET_QK_DIM, RET_V_DIM, RET_HEADS
    return pl.pallas_call(
        _ret_kernel,
        grid=(nh, t // tb),
        in_specs=[pl.BlockSpec((tb, qk), lambda h, i: (i, P_RQ // qk + h)),
                  pl.BlockSpec((tb, qk), lambda h, i: (i, P_RK // qk + h)),
                  pl.BlockSpec((tb, dv), lambda h, i: (i, P_RV // dv + h)),
                  pl.BlockSpec((tb, dv), lambda h, i: (i, P_RG // dv + h)),
                  pl.BlockSpec((tb, qk), lambda h, i: (i, 0)),
                  pl.BlockSpec((tb, qk), lambda h, i: (i, 0))],
        out_specs=pl.BlockSpec((tb, dv), lambda h, i: (i, h)),
        out_shape=jax.ShapeDtypeStruct((t, RET_V), BF16),
        scratch_shapes=[pltpu.VMEM((qk, dv), F32)],
        compiler_params=_params("arbitrary", "arbitrary"),
    )(proj, proj, proj, proj, cos, sin)


def _conformer_kernel(ca_ref, cb_ref, w_ref, b_ref, lnw_ref, lnb_ref, o_ref, upad_ref, sh_ref):
    tb = ca_ref.shape[0]
    rows = tb + HALO

    @pl.when(pl.program_id(0) == 0)
    def _():
        upad_ref[0:HALO, :] = jnp.zeros((HALO, CONV_DIM), F32)

    upad_ref[HALO:, :] = ca_ref[...] * jax.nn.sigmoid(cb_ref[...])
    full = upad_ref[...]
    for r in range(1, SUBLANES):
        sh_ref[r - 1] = pltpu.roll(full, rows - r, axis=0)
    base = HALO - (CONV_WIDTH - 1)
    acc = None
    for j in range(CONV_WIDTH):
        off = base + j
        r, a = off % SUBLANES, off - off % SUBLANES
        src = upad_ref[a:a + tb, :] if r == 0 else sh_ref[r - 1, a:a + tb, :]
        term = w_ref[j:j + 1, :] * src
        acc = term if acc is None else acc + term
    upad_ref[0:HALO, :] = upad_ref[tb:tb + HALO, :]
    u = acc + b_ref[...]
    mu = jnp.mean(u, axis=-1, keepdims=True)
    uc = u - mu
    u = uc * lax.rsqrt(jnp.mean(uc * uc, axis=-1, keepdims=True) + LN_EPS) * lnw_ref[...] + lnb_ref[...]
    o_ref[...] = _silu(u).astype(o_ref.dtype)


def _conformer(proj, w, b, ln_w, ln_b):
    t = proj.shape[0]
    tb = TB_CONV
    c = CONV_DIM
    vec = lambda: pl.BlockSpec((1, c), lambda i: (0, 0))
    return pl.pallas_call(
        _conformer_kernel,
        grid=(t // tb,),
        in_specs=[pl.BlockSpec((tb, c), lambda i: (i, P_GLU // c)),
                  pl.BlockSpec((tb, c), lambda i: (i, P_GLU // c + 1)),
                  pl.BlockSpec((CONV_WIDTH, c), lambda i: (0, 0)),
                  vec(), vec(), vec()],
        out_specs=pl.BlockSpec((tb, c), lambda i: (i, 0)),
        out_shape=jax.ShapeDtypeStruct((t, c), BF16),
        scratch_shapes=[pltpu.VMEM((tb + HALO, c), F32),
                        pltpu.VMEM((SUBLANES - 1, tb + HALO, c), F32)],
        compiler_params=_params("arbitrary"),
    )(proj, proj, w, b.reshape(1, c), ln_w.reshape(1, c), ln_b.reshape(1, c))


def _merge_kernel(oa_ref, ob_ref, oc_ref, wa_ref, wb_ref, wc_ref, ga_ref, gb_ref, gc_ref, o_ref,
                  wa16_ref, wb16_ref, wc16_ref):
    @pl.when(pl.program_id(1) == 0)
    def _():
        wa16_ref[...] = wa_ref[...].astype(BF16)
        wb16_ref[...] = wb_ref[...].astype(BF16)
        wc16_ref[...] = wc_ref[...].astype(BF16)
    m = jax.nn.sigmoid(ga_ref[...]) * _dot(oa_ref[...], wa16_ref[...])
    m = m + jax.nn.sigmoid(gb_ref[...]) * _dot(ob_ref[...], wb16_ref[...])
    m = m + jax.nn.sigmoid(gc_ref[...]) * _dot(oc_ref[...], wc16_ref[...])
    o_ref[...] = m.astype(o_ref.dtype)


def _merge(o_a, o_b, o_c, w_a, w_b, w_c, layer, proj):
    t, k = o_a.shape
    n = w_a.shape[2]
    tm, tn = TM_MERGE, TN
    act = lambda: pl.BlockSpec((tm, k), lambda j, i: (i, 0))
    wsp = lambda: pl.BlockSpec((None, k, tn), lambda j, i: (layer, 0, j))
    gate = lambda b: pl.BlockSpec((tm, tn), lambda j, i: (i, (P_GATE + b * n) // tn + j))
    return pl.pallas_call(
        _merge_kernel,
        grid=(n // tn, t // tm),
        in_specs=[act(), act(), act(), wsp(), wsp(), wsp(), gate(0), gate(1), gate(2)],
        out_specs=pl.BlockSpec((tm, tn), lambda j, i: (i, j)),
        out_shape=jax.ShapeDtypeStruct((t, n), BF16),
        scratch_shapes=[pltpu.VMEM((k, tn), BF16)] * 3,
        compiler_params=_params("arbitrary", "arbitrary"),
    )(o_a, o_b, o_c, w_a, w_b, w_c, proj, proj, proj)


def kernel(x, c, positions, w_ada, b_ada, norm_mix_w, norm_mlp_w, w_in, conv_qkv_w, gdn_a_log, gdn_dt_bias,
           gdn_norm_w, conv_dw_w, conv_dw_b, conv_ln_w, conv_ln_b, w_branch_a, w_branch_b, w_branch_c,
           w_out, w_mlp_in, w_mlp_out, final_norm_w):
    bsz, t, d = x.shape
    assert bsz == 1
    xs = x.reshape(t, d)
    mod = _adaln_mod(c, w_ada, b_ada)
    cos, sin = _rope_tables(positions.reshape(t))
    w_in_t = jnp.swapaxes(w_in, 1, 2)
    for l in range(w_in.shape[0]):
        h = _norm_mod(xs, norm_mix_w[l], mod[l], 0, 1)
        proj = _in_proj(h, w_in_t, l)
        bg, bgt = _gdn_gates(h, w_in_t, l, gdn_a_log[l], gdn_dt_bias[l])
        o_a = _gdn(proj, bg, bgt, conv_qkv_w[l], gdn_norm_w[l])
        o_b = _retention(proj, cos, sin)
        o_c = _conformer(proj, conv_dw_w[l], conv_dw_b[l], conv_ln_w[l], conv_ln_b[l])
        merged = _merge(o_a, o_b, o_c, w_branch_a, w_branch_b, w_branch_c, l, proj)
        xs = _matmul_residual(merged, w_out, l, xs, mod[l, 2:3])
        h = _norm_mod(xs, norm_mlp_w[l], mod[l], 3, 4)
        act = _matmul(h, w_mlp_in, l, out_dtype=BF16, relu2=True)
        xs = _mlp_out(act, w_mlp_out, l, xs, mod[l, 5:6])
    return _final_norm(xs, final_norm_w).reshape(bsz, t, d)
```

```python
import functools
import math

import jax
import jax.numpy as jnp
from jax import lax
from jax.experimental import pallas as pl
from jax.experimental.pallas import tpu as pltpu

F32 = jnp.float32
BF16 = jnp.bfloat16

D_MODEL = 2048
GDN_HEADS = 8
GDN_HEAD_DIM = 128
GDN_DIM = GDN_HEADS * GDN_HEAD_DIM
SHORT_CONV = 4
RET_HEADS = 4
RET_QK_DIM = 128
RET_V_DIM = 256
RET_QK = RET_HEADS * RET_QK_DIM
RET_V = RET_HEADS * RET_V_DIM
RET_DECAY_BASE = 5.0
ROPE_BASE = 10000.0
CONV_DIM = D_MODEL // 2
CONV_WIDTH = 31
N_BRANCH = 3
NORM_EPS = 1e-6
LN_EPS = 1e-5
L2_EPS = 1e-6

BA_COLS = 2 * GDN_HEADS
P_GDN = 0
P_RQ = 4 * GDN_DIM
P_RK = P_RQ + RET_QK
P_RV = P_RK + RET_QK
P_RG = P_RV + RET_V
P_GLU = P_RG + RET_V
P_GATE = P_GLU + 2 * CONV_DIM
P_WIDTH = P_GATE + N_BRANCH * D_MODEL

LANES = 128
SUBLANES = 8
VMEM_LIMIT_BYTES = 56 * 1024 * 1024
TM = 1024
TM_MERGE = 512
TN = 1024
GDN_CHUNK = 128
TB_GATE = 512
TB_RET = 256
TB_CONV = 256
TB_NORM = 512
HALO = 32


def _params(*sem):
    return pltpu.CompilerParams(dimension_semantics=sem, vmem_limit_bytes=VMEM_LIMIT_BYTES)


def _dot(a, b):
    return jnp.dot(a, b, preferred_element_type=F32)


def _dot_nt(a, b):
    return lax.dot_general(a, b, (((1,), (1,)), ((), ())), preferred_element_type=F32)


def _dot_exact_lhs(m_bf16, g):
    g1 = g.astype(BF16)
    r1 = g - g1.astype(F32)
    g2 = r1.astype(BF16)
    g3 = (r1 - g2.astype(F32)).astype(BF16)
    return _dot(m_bf16, g1) + _dot(m_bf16, g2) + _dot(m_bf16, g3)


def _silu(x):
    return x * jax.nn.sigmoid(x)


def _mod_kernel(c_ref, w_ref, b_ref, o_ref):
    prod = _silu(c_ref[...]) * w_ref[0]
    d, tn = prod.shape
    part = jnp.sum(prod.reshape(d // SUBLANES, SUBLANES, tn), axis=0)
    o_ref[0] = jnp.sum(part, axis=0, keepdims=True) + b_ref[0]


def _adaln_mod(c, w_ada, b_ada):
    depth, d, n = w_ada.shape
    tn = 1024
    out = pl.pallas_call(
        _mod_kernel,
        grid=(depth, n // tn),
        in_specs=[pl.BlockSpec((d, 1), lambda l, j: (0, 0)),
                  pl.BlockSpec((1, d, tn), lambda l, j: (l, 0, j)),
                  pl.BlockSpec((1, 1, tn), lambda l, j: (l, 0, j))],
        out_specs=pl.BlockSpec((1, 1, tn), lambda l, j: (l, 0, j)),
        out_shape=jax.ShapeDtypeStruct((depth, 1, n), F32),
        compiler_params=_params("arbitrary", "arbitrary"),
    )(c.reshape(d, 1), w_ada, b_ada.reshape(depth, 1, n))
    return out.reshape(depth, 6, d)


def _norm_mod_kernel(x_ref, w_ref, mod_ref, o_ref, *, shift_row, scale_row):
    x = x_ref[...]
    y = x * lax.rsqrt(jnp.mean(x * x, axis=-1, keepdims=True) + NORM_EPS) * w_ref[...]
    y = y * (1.0 + mod_ref[scale_row:scale_row + 1, :]) + mod_ref[shift_row:shift_row + 1, :]
    o_ref[...] = y.astype(o_ref.dtype)


def _norm_mod(x, w, mod, shift_row, scale_row):
    t, d = x.shape
    return pl.pallas_call(
        functools.partial(_norm_mod_kernel, shift_row=shift_row, scale_row=scale_row),
        grid=(t // TB_NORM,),
        in_specs=[pl.BlockSpec((TB_NORM, d), lambda i: (i, 0)),
                  pl.BlockSpec((1, d), lambda i: (0, 0)),
                  pl.BlockSpec((6, d), lambda i: (0, 0))],
        out_specs=pl.BlockSpec((TB_NORM, d), lambda i: (i, 0)),
        out_shape=jax.ShapeDtypeStruct((t, d), BF16),
        compiler_params=_params("arbitrary"),
    )(x, w.reshape(1, d), mod)


def _final_norm_kernel(x_ref, w_ref, o_ref):
    x = x_ref[...]
    o_ref[...] = x * lax.rsqrt(jnp.mean(x * x, axis=-1, keepdims=True) + NORM_EPS) * w_ref[...]


def _final_norm(x, w):
    t, d = x.shape
    return pl.pallas_call(
        _final_norm_kernel,
        grid=(t // TB_NORM,),
        in_specs=[pl.BlockSpec((TB_NORM, d), lambda i: (i, 0)),
                  pl.BlockSpec((1, d), lambda i: (0, 0))],
        out_specs=pl.BlockSpec((TB_NORM, d), lambda i: (i, 0)),
        out_shape=jax.ShapeDtypeStruct((t, d), F32),
        compiler_params=_params("arbitrary"),
    )(x, w.reshape(1, d))


def _mm_kernel(a_ref, w_ref, o_ref, wb_ref):
    @pl.when(pl.program_id(1) == 0)
    def _():
        wb_ref[...] = w_ref[...].astype(BF16)
    o_ref[...] = _dot(a_ref[...], wb_ref[...]).astype(o_ref.dtype)


def _mm_relu2_kernel(a_ref, w_ref, o_ref, wb_ref):
    @pl.when(pl.program_id(1) == 0)
    def _():
        wb_ref[...] = w_ref[...].astype(BF16)
    y = jnp.maximum(_dot(a_ref[...], wb_ref[...]), 0.0)
    o_ref[...] = (y * y).astype(o_ref.dtype)


def _mm_residual_kernel(a_ref, w_ref, x_ref, g_ref, o_ref, wb_ref):
    @pl.when(pl.program_id(1) == 0)
    def _():
        wb_ref[...] = w_ref[...].astype(BF16)
    o_ref[...] = x_ref[...] + g_ref[...] * _dot(a_ref[...], wb_ref[...])


def _matmul(a, w, layer, *, out_dtype=F32, relu2=False):
    t, k = a.shape
    n = w.shape[2]
    tn = min(TN, n)
    return pl.pallas_call(
        _mm_relu2_kernel if relu2 else _mm_kernel,
        grid=(n // tn, t // TM),
        in_specs=[pl.BlockSpec((TM, k), lambda j, i: (i, 0)),
                  pl.BlockSpec((None, k, tn), lambda j, i: (layer, 0, j))],
        out_specs=pl.BlockSpec((TM, tn), lambda j, i: (i, j)),
        out_shape=jax.ShapeDtypeStruct((t, n), out_dtype),
        scratch_shapes=[pltpu.VMEM((k, tn), BF16)],
        compiler_params=_params("arbitrary", "arbitrary"),
    )(a, w)


def _matmul_residual(a, w, layer, x, gate):
    t, k = a.shape
    n = w.shape[2]
    tn = min(TN, n)
    return pl.pallas_call(
        _mm_residual_kernel,
        grid=(n // tn, t // TM),
        in_specs=[pl.BlockSpec((TM, k), lambda j, i: (i, 0)),
                  pl.BlockSpec((None, k, tn), lambda j, i: (layer, 0, j)),
                  pl.BlockSpec((TM, tn), lambda j, i: (i, j)),
                  pl.BlockSpec((1, tn), lambda j, i: (0, j))],
        out_specs=pl.BlockSpec((TM, tn), lambda j, i: (i, j)),
        out_shape=jax.ShapeDtypeStruct((t, n), F32),
        scratch_shapes=[pltpu.VMEM((k, tn), BF16)],
        compiler_params=_params("arbitrary", "arbitrary"),
    )(a, w, x, gate)


def _in_proj_kernel(a_ref, w_ref, wn_ref, o_ref, wb_ref, *, plain_tiles):
    j = pl.program_id(0)
    first = pl.program_id(1) == 0

    @pl.when(first & (j < plain_tiles))
    def _():
        wb_ref[...] = w_ref[...].astype(BF16)

    @pl.when(first & (j >= plain_tiles))
    def _():
        keep = w_ref.shape[0] - BA_COLS
        wb_ref[0:keep, :] = w_ref[BA_COLS:, :].astype(BF16)
        wb_ref[keep:, :] = wn_ref[...].astype(BF16)

    o_ref[...] = _dot_nt(a_ref[...], wb_ref[...])


def _in_proj(a, w_in_t, layer):
    t, k = a.shape
    tn = TN
    assert P_RQ % tn == 0 and P_WIDTH % tn == 0 and w_in_t.shape[1] == P_WIDTH + BA_COLS
    return pl.pallas_call(
        functools.partial(_in_proj_kernel, plain_tiles=P_RQ // tn),
        grid=(P_WIDTH // tn, t // TM),
        in_specs=[pl.BlockSpec((TM, k), lambda j, i: (i, 0)),
                  pl.BlockSpec((None, tn, k), lambda j, i: (layer, j, 0)),
                  pl.BlockSpec((None, BA_COLS, k), lambda j, i: (layer, (j + 1) * (tn // BA_COLS), 0))],
        out_specs=pl.BlockSpec((TM, tn), lambda j, i: (i, j)),
        out_shape=jax.ShapeDtypeStruct((t, P_WIDTH), F32),
        scratch_shapes=[pltpu.VMEM((tn, k), BF16)],
        compiler_params=_params("arbitrary", "arbitrary"),
    )(a, w_in_t, w_in_t)


def _mlp_out_kernel(a_ref, w_ref, x_ref, g_ref, o_ref, acc_ref):
    kk = pl.program_id(2)

    @pl.when(kk == 0)
    def _():
        acc_ref[...] = jnp.zeros_like(acc_ref)
    acc_ref[...] += _dot(a_ref[...], w_ref[...].astype(BF16))

    @pl.when(kk == pl.num_programs(2) - 1)
    def _():
        o_ref[...] = x_ref[...] + g_ref[...] * acc_ref[...]


def _mlp_out(a, w, layer, x, gate):
    t, k = a.shape
    n = w.shape[2]
    tm, tn, tk = 1024, 1024, 1024
    return pl.pallas_call(
        _mlp_out_kernel,
        grid=(t // tm, n // tn, k // tk),
        in_specs=[pl.BlockSpec((tm, tk), lambda i, j, kk: (i, kk)),
                  pl.BlockSpec((None, tk, tn), lambda i, j, kk: (layer, kk, j)),
                  pl.BlockSpec((tm, tn), lambda i, j, kk: (i, j)),
                  pl.BlockSpec((1, tn), lambda i, j, kk: (0, j))],
        out_specs=pl.BlockSpec((tm, tn), lambda i, j, kk: (i, j)),
        out_shape=jax.ShapeDtypeStruct((t, n), F32),
        scratch_shapes=[pltpu.VMEM((tm, tn), F32)],
        compiler_params=_params("arbitrary", "arbitrary", "arbitrary"),
    )(a, w, x, gate)


def _gdn_gate_kernel(h_ref, w_ref, alog_ref, dtb_ref, bg_ref, bgt_ref):
    tb = h_ref.shape[0]
    ba = _dot_nt(h_ref[...], w_ref[...].astype(BF16))
    beta = jax.nn.sigmoid(ba)
    xs = ba + dtb_ref[...]
    softplus = jnp.maximum(xs, 0.0) + jnp.log1p(jnp.exp(-jnp.abs(xs)))
    g = -jnp.exp(alog_ref[...]) * softplus
    ck = GDN_CHUNK
    ii = lax.broadcasted_iota(jnp.int32, (ck, ck), 0)
    jj = lax.broadcasted_iota(jnp.int32, (ck, ck), 1)
    tri = jnp.where(ii >= jj, 1.0, 0.0).astype(BF16)
    ones = jnp.ones((ck, ck), BF16)
    chunks = [g[s * ck:(s + 1) * ck] for s in range(tb // ck)]
    gc = jnp.concatenate([_dot_exact_lhs(tri, gs) for gs in chunks], axis=0)
    gl = jnp.concatenate([_dot_exact_lhs(ones, gs) for gs in chunks], axis=0)
    lane = lax.broadcasted_iota(jnp.int32, (tb, LANES), 1)
    out = jnp.where(lane < GDN_HEADS, beta,
                    jnp.where(lane < 2 * GDN_HEADS, gc,
                              jnp.where(lane < 3 * GDN_HEADS, pltpu.roll(gl, GDN_HEADS, axis=1), 0.0)))
    bg_ref[...] = out
    bgt_ref[...] = out.T


def _gdn_gates(h, w_in_t, layer, a_log, dt_bias):
    t, d = h.shape
    tb = TB_GATE
    lead = jnp.zeros((GDN_HEADS,), F32)
    tail = jnp.zeros((LANES - 2 * GDN_HEADS,), F32)
    alog = jnp.concatenate([lead, a_log, tail]).reshape(1, LANES)
    dtb = jnp.concatenate([lead, dt_bias, tail]).reshape(1, LANES)
    return pl.pallas_call(
        _gdn_gate_kernel,
        grid=(t // tb,),
        in_specs=[pl.BlockSpec((tb, d), lambda i: (i, 0)),
                  pl.BlockSpec((None, LANES, d), lambda i: (layer, P_RQ // LANES, 0)),
                  pl.BlockSpec((1, LANES), lambda i: (0, 0)),
                  pl.BlockSpec((1, LANES), lambda i: (0, 0))],
        out_specs=[pl.BlockSpec((tb, LANES), lambda i: (i, 0)),
                   pl.BlockSpec((LANES, tb), lambda i: (0, i))],
        out_shape=[jax.ShapeDtypeStruct((t, LANES), F32),
                   jax.ShapeDtypeStruct((LANES, t), F32)],
        compiler_params=_params("arbitrary"),
    )(h, w_in_t, alog, dtb)


def _tri_inverse_all(lows):
    n = lows[0].shape[0]
    ii = lax.broadcasted_iota(jnp.int32, (n, n), 0)
    jj = lax.broadcasted_iota(jnp.int32, (n, n), 1)
    ts = None
    for level in range(int(math.log2(n))):
        rb = jnp.right_shift(ii, level)
        cb = jnp.right_shift(jj, level)
        sel = (jnp.bitwise_and(rb, 1) == 1) & (cb == rb - 1)
        if level == 0:
            eye = jnp.where(ii == jj, 1.0, 0.0)
            ts = [eye - jnp.where(sel, low, 0.0) for low in lows]
        else:
            t16 = [t.astype(BF16) for t in ts]
            ys = [_dot(jnp.where(sel, low, 0.0).astype(BF16), t).astype(BF16) for low, t in zip(lows, t16)]
            ts = [t - _dot(tb16, y) for t, tb16, y in zip(ts, t16, ys)]
    return ts


def _gdn_kernel(q_ref, k_ref, v_ref, z_ref, bg_ref, bgt_ref, cw_ref, nw_ref, o_ref, xpad_ref, s_ref):
    tb = q_ref.shape[0]
    hd = GDN_HEAD_DIM

    @pl.when(pl.program_id(0) == 0)
    def _():
        xpad_ref[:, 0:SUBLANES, :] = jnp.zeros((3, SUBLANES, GDN_DIM), F32)
        s_ref[...] = jnp.zeros_like(s_ref)

    xpad_ref[0, SUBLANES:, :] = q_ref[...]
    xpad_ref[1, SUBLANES:, :] = k_ref[...]
    xpad_ref[2, SUBLANES:, :] = v_ref[...]

    def conv(i):
        cols = slice(i * GDN_DIM, (i + 1) * GDN_DIM)
        full = xpad_ref[i]
        acc = cw_ref[SHORT_CONV - 1:SHORT_CONV, cols] * full[SUBLANES:]
        for j in range(SHORT_CONV - 1):
            shifted = pltpu.roll(full, SHORT_CONV - 1 - j, axis=0)
            acc = acc + cw_ref[j:j + 1, cols] * shifted[SUBLANES:]
        return _silu(acc)

    q_all, k_all, v_all = conv(0), conv(1), conv(2)
    xpad_ref[:, 0:SUBLANES, :] = xpad_ref[:, tb:tb + SUBLANES, :]

    bg = bg_ref[...]
    bgt = bgt_ref[...]
    ii = lax.broadcasted_iota(jnp.int32, (tb, tb), 0)
    jj = lax.broadcasted_iota(jnp.int32, (tb, tb), 1)
    causal = ii >= jj
    strict = ii > jj
    nw = nw_ref[...]
    heads = range(GDN_HEADS)
    sl = [slice(h * hd, (h + 1) * hd) for h in heads]

    l2n = lambda x: x * lax.rsqrt(jnp.sum(x * x, axis=-1, keepdims=True) + L2_EPS)
    q = [l2n(q_all[:, sl[h]]) * (hd ** -0.5) for h in heads]
    k = [l2n(k_all[:, sl[h]]) for h in heads]
    beta = [bg[:, h:h + 1] for h in heads]
    gc = [bg[:, GDN_HEADS + h:GDN_HEADS + h + 1] for h in heads]
    gl = [bg[:, 2 * GDN_HEADS + h:2 * GDN_HEADS + h + 1] for h in heads]
    eg = [jnp.exp(gc[h]) for h in heads]
    decay = [jnp.exp(jnp.where(causal, gc[h] - bgt[h:h + 1, :], -jnp.inf)) for h in heads]
    kb = [k[h] * beta[h] for h in heads]
    kq = [_dot_nt(jnp.concatenate([kb[h], q[h]], axis=0).astype(BF16), k[h].astype(BF16)) for h in heads]
    low = [jnp.where(strict, kq[h][:tb] * decay[h], 0.0) for h in heads]
    attn = [(kq[h][tb:] * decay[h]).astype(BF16) for h in heads]
    t_inv = _tri_inverse_all(low)
    rhs = [jnp.concatenate([v_all[:, sl[h]] * beta[h], kb[h] * eg[h]], axis=1).astype(BF16) for h in heads]
    uw = [_dot(t_inv[h].astype(BF16), rhs[h]) for h in heads]
    state = [s_ref[h] for h in heads]
    ws = [_dot(jnp.concatenate([uw[h][:, hd:], q[h] * eg[h]], axis=0).astype(BF16), state[h].astype(BF16))
          for h in heads]
    v_new = [(uw[h][:, :hd] - ws[h][:tb]).astype(BF16) for h in heads]
    k_dec_t = [(k[h] * jnp.exp(gl[h] - gc[h])).T.astype(BF16) for h in heads]
    for h in heads:
        egl = jnp.broadcast_to(jnp.exp(gl[h]), (tb, hd))[0:1, :]
        s_ref[h] = state[h] * egl + _dot(k_dec_t[h], v_new[h])
    o = [ws[h][tb:] + _dot(attn[h], v_new[h]) for h in heads]
    o = [o[h] * lax.rsqrt(jnp.mean(o[h] * o[h], axis=-1, keepdims=True) + NORM_EPS) * nw for h in heads]
    o_ref[...] = jnp.concatenate([(o[h] * _silu(z_ref[:, sl[h]])).astype(o_ref.dtype) for h in heads], axis=1)


def _gdn(proj, bg, bgt, conv_w, norm_w):
    t = proj.shape[0]
    tb = GDN_CHUNK
    gd, hd = GDN_DIM, GDN_HEAD_DIM
    col = lambda c: pl.BlockSpec((tb, gd), lambda i: (i, P_GDN // gd + c))
    return pl.pallas_call(
        _gdn_kernel,
        grid=(t // tb,),
        in_specs=[col(0), col(1), col(2), col(3),
                  pl.BlockSpec((tb, LANES), lambda i: (i, 0)),
                  pl.BlockSpec((SUBLANES, tb), lambda i: (1, i)),
                  pl.BlockSpec((SHORT_CONV, 3 * gd), lambda i: (0, 0)),
                  pl.BlockSpec((1, hd), lambda i: (0, 0))],
        out_specs=pl.BlockSpec((tb, gd), lambda i: (i, 0)),
        out_shape=jax.ShapeDtypeStruct((t, gd), BF16),
        scratch_shapes=[pltpu.VMEM((3, tb + SUBLANES, gd), F32),
                        pltpu.VMEM((GDN_HEADS, hd, hd), F32)],
        compiler_params=_params("arbitrary"),
    )(proj, proj, proj, proj, bg, bgt, conv_w, norm_w.reshape(1, hd))


def _rope_kernel(pos_ref, cos_ref, sin_ref):
    half = RET_QK_DIM // 2
    lane = lax.broadcasted_iota(jnp.int32, (1, RET_QK_DIM), 1)
    idx = jnp.where(lane < half, lane, lane - half).astype(F32)
    inv_freq = jnp.exp(idx * (-math.log(ROPE_BASE) / half))
    ang = pos_ref[...].astype(F32) * inv_freq
    cos_ref[...] = jnp.cos(ang)
    sin_ref[...] = jnp.where(lane < half, -1.0, 1.0) * jnp.sin(ang)


def _rope_tables(positions):
    t = positions.shape[0]
    tb = 1024
    return pl.pallas_call(
        _rope_kernel,
        grid=(t // tb,),
        in_specs=[pl.BlockSpec((tb, 1), lambda i: (i, 0))],
        out_specs=[pl.BlockSpec((tb, RET_QK_DIM), lambda i: (i, 0))] * 2,
        out_shape=[jax.ShapeDtypeStruct((t, RET_QK_DIM), F32)] * 2,
        compiler_params=_params("arbitrary"),
    )(positions.reshape(t, 1))


def _ret_kernel(q_ref, k_ref, v_ref, g_ref, cos_ref, sin_ref, o_ref, r_ref, dmask_ref, qs_ref, ks_ref):
    tb = q_ref.shape[0]
    qk, dv = RET_QK_DIM, RET_V_DIM
    heads = range(RET_HEADS)
    log_gamma = [math.log(1.0 - 2.0 ** (-RET_DECAY_BASE - h)) for h in heads]

    @pl.when(pl.program_id(0) == 0)
    def _():
        r_ref[...] = jnp.zeros_like(r_ref)
        ii = lax.broadcasted_iota(jnp.int32, (tb, tb), 0)
        jj = lax.broadcasted_iota(jnp.int32, (tb, tb), 1)
        rel = (ii - jj).astype(F32)
        idx = lax.broadcasted_iota(jnp.int32, (tb, qk), 0).astype(F32)
        for h in heads:
            dmask_ref[h] = jnp.where(rel >= 0.0, jnp.exp(jnp.maximum(rel, 0.0) * log_gamma[h]), 0.0)
            qs_ref[h] = jnp.exp((idx + 1.0) * log_gamma[h])
            ks_ref[h] = jnp.exp((tb - 1.0 - idx) * log_gamma[h])

    cos, sin = cos_ref[...], sin_ref[...]
    rope = lambda x: x * cos + pltpu.roll(x, qk // 2, axis=1) * sin
    q = [rope(q_ref[:, h * qk:(h + 1) * qk]) for h in heads]
    k = [rope(k_ref[:, h * qk:(h + 1) * qk]) * (qk ** -0.5) for h in heads]
    v16 = [v_ref[:, h * dv:(h + 1) * dv].astype(BF16) for h in heads]
    scores = [(_dot_nt(q[h].astype(BF16), k[h].astype(BF16)) * dmask_ref[h]).astype(BF16) for h in heads]
    state = [r_ref[h] for h in heads]
    o = [_dot(scores[h], v16[h]) + _dot((q[h] * qs_ref[h]).astype(BF16), state[h].astype(BF16)) for h in heads]
    for h in heads:
        r_ref[h] = state[h] * math.exp(tb * log_gamma[h]) + _dot((k[h] * ks_ref[h]).T.astype(BF16), v16[h])
    outs = []
    for h in heads:
        oc = o[h] - jnp.mean(o[h], axis=-1, keepdims=True)
        on = oc * lax.rsqrt(jnp.mean(oc * oc, axis=-1, keepdims=True) + LN_EPS)
        outs.append((on * _silu(g_ref[:, h * dv:(h + 1) * dv])).astype(o_ref.dtype))
    o_ref[...] = jnp.concatenate(outs, axis=1)


def _retention(proj, cos, sin):
    t = proj.shape[0]
    tb = TB_RET
    qk, dv, nh = RET_QK_DIM, RET_V_DIM, RET_HEADS
    return pl.pallas_call(
        _ret_kernel,
        grid=(t // tb,),
        in_specs=[pl.BlockSpec((tb, RET_QK), lambda i: (i, P_RQ // RET_QK)),
                  pl.BlockSpec((tb, RET_QK), lambda i: (i, P_RK // RET_QK)),
                  pl.BlockSpec((tb, RET_V), lambda i: (i, P_RV // RET_V)),
                  pl.BlockSpec((tb, RET_V), lambda i: (i, P_RG // RET_V)),
                  pl.BlockSpec((tb, qk), lambda i: (i, 0)),
                  pl.BlockSpec((tb, qk), lambda i: (i, 0))],
        out_specs=pl.BlockSpec((tb, RET_V), lambda i: (i, 0)),
        out_shape=jax.ShapeDtypeStruct((t, RET_V), BF16),
        scratch_shapes=[pltpu.VMEM((nh, qk, dv), F32),
                        pltpu.VMEM((nh, tb, tb), F32),
                        pltpu.VMEM((nh, tb, qk), F32),
                        pltpu.VMEM((nh, tb, qk), F32)],
        compiler_params=_params("arbitrary"),
    )(proj, proj, proj, proj, cos, sin)


def _conformer_kernel(ca_ref, cb_ref, w_ref, b_ref, lnw_ref, lnb_ref, o_ref, upad_ref, sh_ref):
    tb = ca_ref.shape[0]
    rows = tb + HALO

    @pl.when(pl.program_id(0) == 0)
    def _():
        upad_ref[0:HALO, :] = jnp.zeros((HALO, CONV_DIM), F32)

    upad_ref[HALO:, :] = ca_ref[...] * jax.nn.sigmoid(cb_ref[...])
    full = upad_ref[...]
    for r in range(1, SUBLANES):
        sh_ref[r - 1] = pltpu.roll(full, rows - r, axis=0)
    base = HALO - (CONV_WIDTH - 1)
    acc = None
    for j in range(CONV_WIDTH):
        off = base + j
        r, a = off % SUBLANES, off - off % SUBLANES
        src = upad_ref[a:a + tb, :] if r == 0 else sh_ref[r - 1, a:a + tb, :]
        term = w_ref[j:j + 1, :] * src
        acc = term if acc is None else acc + term
    upad_ref[0:HALO, :] = upad_ref[tb:tb + HALO, :]
    u = acc + b_ref[...]
    mu = jnp.mean(u, axis=-1, keepdims=True)
    uc = u - mu
    u = uc * lax.rsqrt(jnp.mean(uc * uc, axis=-1, keepdims=True) + LN_EPS) * lnw_ref[...] + lnb_ref[...]
    o_ref[...] = _silu(u).astype(o_ref.dtype)


def _conformer(proj, w, b, ln_w, ln_b):
    t = proj.shape[0]
    tb = TB_CONV
    c = CONV_DIM
    vec = lambda: pl.BlockSpec((1, c), lambda i: (0, 0))
    return pl.pallas_call(
        _conformer_kernel,
        grid=(t // tb,),
        in_specs=[pl.BlockSpec((tb, c), lambda i: (i, P_GLU // c)),
                  pl.BlockSpec((tb, c), lambda i: (i, P_GLU // c + 1)),
                  pl.BlockSpec((CONV_WIDTH, c), lambda i: (0, 0)),
                  vec(), vec(), vec()],
        out_specs=pl.BlockSpec((tb, c), lambda i: (i, 0)),
        out_shape=jax.ShapeDtypeStruct((t, c), BF16),
        scratch_shapes=[pltpu.VMEM((tb + HALO, c), F32),
                        pltpu.VMEM((SUBLANES - 1, tb + HALO, c), F32)],
        compiler_params=_params("arbitrary"),
    )(proj, proj, w, b.reshape(1, c), ln_w.reshape(1, c), ln_b.reshape(1, c))


def _merge_kernel(oa_ref, ob_ref, oc_ref, wa_ref, wb_ref, wc_ref, ga_ref, gb_ref, gc_ref, o_ref,
                  wa16_ref, wb16_ref, wc16_ref):
    @pl.when(pl.program_id(1) == 0)
    def _():
        wa16_ref[...] = wa_ref[...].astype(BF16)
        wb16_ref[...] = wb_ref[...].astype(BF16)
        wc16_ref[...] = wc_ref[...].astype(BF16)
    m = jax.nn.sigmoid(ga_ref[...]) * _dot(oa_ref[...], wa16_ref[...])
    m = m + jax.nn.sigmoid(gb_ref[...]) * _dot(ob_ref[...], wb16_ref[...])
    m = m + jax.nn.sigmoid(gc_ref[...]) * _dot(oc_ref[...], wc16_ref[...])
    o_ref[...] = m.astype(o_ref.dtype)


def _merge(o_a, o_b, o_c, w_a, w_b, w_c, layer, proj):
    t, k = o_a.shape
    n = w_a.shape[2]
    tm, tn = TM_MERGE, TN
    act = lambda: pl.BlockSpec((tm, k), lambda j, i: (i, 0))
    wsp = lambda: pl.BlockSpec((None, k, tn), lambda j, i: (layer, 0, j))
    gate = lambda b: pl.BlockSpec((tm, tn), lambda j, i: (i, (P_GATE + b * n) // tn + j))
    return pl.pallas_call(
        _merge_kernel,
        grid=(n // tn, t // tm),
        in_specs=[act(), act(), act(), wsp(), wsp(), wsp(), gate(0), gate(1), gate(2)],
        out_specs=pl.BlockSpec((tm, tn), lambda j, i: (i, j)),
        out_shape=jax.ShapeDtypeStruct((t, n), BF16),
        scratch_shapes=[pltpu.VMEM((k, tn), BF16)] * 3,
        compiler_params=_params("arbitrary", "arbitrary"),
    )(o_a, o_b, o_c, w_a, w_b, w_c, proj, proj, proj)


def kernel(x, c, positions, w_ada, b_ada, norm_mix_w, norm_mlp_w, w_in, conv_qkv_w, gdn_a_log, gdn_dt_bias,
           gdn_norm_w, conv_dw_w, conv_dw_b, conv_ln_w, conv_ln_b, w_branch_a, w_branch_b, w_branch_c,
           w_out, w_mlp_in, w_mlp_out, final_norm_w):
    bsz, t, d = x.shape
    assert bsz == 1
    xs = x.reshape(t, d)
    mod = _adaln_mod(c, w_ada, b_ada)
    cos, sin = _rope_tables(positions.reshape(t))
    w_in_t = jnp.swapaxes(w_in, 1, 2)
    for l in range(w_in.shape[0]):
        h = _norm_mod(xs, norm_mix_w[l], mod[l], 0, 1)
        proj = _in_proj(h, w_in_t, l)
        bg, bgt = _gdn_gates(h, w_in_t, l, gdn_a_log[l], gdn_dt_bias[l])
        o_a = _gdn(proj, bg, bgt, conv_qkv_w[l], gdn_norm_w[l])
        o_b = _retention(proj, cos, sin)
        o_c = _conformer(proj, conv_dw_w[l], conv_dw_b[l], conv_ln_w[l], conv_ln_b[l])
        merged = _merge(o_a, o_b, o_c, w_branch_a, w_branch_b, w_branch_c, l, proj)
        xs = _matmul_residual(merged, w_out, l, xs, mod[l, 2:3])
        h = _norm_mod(xs, norm_mlp_w[l], mod[l], 3, 4)
        act = _matmul(h, w_mlp_in, l, out_dtype=BF16, relu2=True)
        xs = _mlp_out(act, w_mlp_out, l, xs, mod[l, 5:6])
    return _final_norm(xs, final_norm_w).reshape(bsz, t, d)
```

```python
import functools
import math

import jax
import jax.numpy as jnp
from jax import lax
from jax.experimental import pallas as pl
from jax.experimental.pallas import tpu as pltpu

F32 = jnp.float32
BF16 = jnp.bfloat16

D_MODEL = 2048
GDN_HEADS = 8
GDN_HEAD_DIM = 128
GDN_DIM = GDN_HEADS * GDN_HEAD_DIM
SHORT_CONV = 4
RET_HEADS = 4
RET_QK_DIM = 128
RET_V_DIM = 256
RET_QK = RET_HEADS * RET_QK_DIM
RET_V = RET_HEADS * RET_V_DIM
RET_DECAY_BASE = 5.0
ROPE_BASE = 10000.0
CONV_DIM = D_MODEL // 2
CONV_WIDTH = 31
N_BRANCH = 3
NORM_EPS = 1e-6
LN_EPS = 1e-5
L2_EPS = 1e-6

BA_COLS = 2 * GDN_HEADS
P_RQ = 4 * GDN_DIM
R_RQ = 0
R_RK = R_RQ + RET_QK
R_RV = R_RK + RET_QK
R_RG = R_RV + RET_V
R_GLU = R_RG + RET_V
R_GATE = R_GLU + 2 * CONV_DIM
R_WIDTH = R_GATE + N_BRANCH * D_MODEL

LANES = 128
SUBLANES = 8
VMEM_LIMIT_BYTES = 56 * 1024 * 1024
TM = 1024
TM_WIDE = 2048
TM_MERGE = 512
TN = 1024
ROW_CHUNK = 256
GDN_CHUNK = 128
TB_GATE = 512
TB_RET = 256
TB_CONV = 256
TB_NORM = 512
HALO = 32


def _params(*sem):
    return pltpu.CompilerParams(dimension_semantics=sem, vmem_limit_bytes=VMEM_LIMIT_BYTES)


def _dot(a, b):
    return jnp.dot(a, b, preferred_element_type=F32)


def _dot_nt(a, b):
    return lax.dot_general(a, b, (((1,), (1,)), ((), ())), preferred_element_type=F32)


def _dot_exact_lhs(m_bf16, g):
    g1 = g.astype(BF16)
    r1 = g - g1.astype(F32)
    g2 = r1.astype(BF16)
    g3 = (r1 - g2.astype(F32)).astype(BF16)
    return _dot(m_bf16, g1) + _dot(m_bf16, g2) + _dot(m_bf16, g3)


def _silu(x):
    return x * jax.nn.sigmoid(x)


def _mod_kernel(c_ref, w_ref, b_ref, o_ref):
    prod = _silu(c_ref[...]) * w_ref[0]
    d, tn = prod.shape
    part = jnp.sum(prod.reshape(d // SUBLANES, SUBLANES, tn), axis=0)
    o_ref[0] = jnp.sum(part, axis=0, keepdims=True) + b_ref[0]


def _adaln_mod(c, w_ada, b_ada):
    depth, d, n = w_ada.shape
    tn = 1024
    out = pl.pallas_call(
        _mod_kernel,
        grid=(depth, n // tn),
        in_specs=[pl.BlockSpec((d, 1), lambda l, j: (0, 0)),
                  pl.BlockSpec((1, d, tn), lambda l, j: (l, 0, j)),
                  pl.BlockSpec((1, 1, tn), lambda l, j: (l, 0, j))],
        out_specs=pl.BlockSpec((1, 1, tn), lambda l, j: (l, 0, j)),
        out_shape=jax.ShapeDtypeStruct((depth, 1, n), F32),
        compiler_params=_params("arbitrary", "arbitrary"),
    )(c.reshape(d, 1), w_ada, b_ada.reshape(depth, 1, n))
    return out.reshape(depth, 6, d)


def _norm_mod_kernel(x_ref, w_ref, mod_ref, o_ref, *, shift_row, scale_row):
    x = x_ref[...]
    y = x * lax.rsqrt(jnp.mean(x * x, axis=-1, keepdims=True) + NORM_EPS) * w_ref[...]
    y = y * (1.0 + mod_ref[scale_row:scale_row + 1, :]) + mod_ref[shift_row:shift_row + 1, :]
    o_ref[...] = y.astype(o_ref.dtype)


def _norm_mod(x, w, mod, shift_row, scale_row):
    t, d = x.shape
    return pl.pallas_call(
        functools.partial(_norm_mod_kernel, shift_row=shift_row, scale_row=scale_row),
        grid=(t // TB_NORM,),
        in_specs=[pl.BlockSpec((TB_NORM, d), lambda i: (i, 0)),
                  pl.BlockSpec((1, d), lambda i: (0, 0)),
                  pl.BlockSpec((6, d), lambda i: (0, 0))],
        out_specs=pl.BlockSpec((TB_NORM, d), lambda i: (i, 0)),
        out_shape=jax.ShapeDtypeStruct((t, d), BF16),
        compiler_params=_params("arbitrary"),
    )(x, w.reshape(1, d), mod)


def _final_norm_kernel(x_ref, w_ref, o_ref):
    x = x_ref[...]
    o_ref[...] = x * lax.rsqrt(jnp.mean(x * x, axis=-1, keepdims=True) + NORM_EPS) * w_ref[...]


def _final_norm(x, w):
    t, d = x.shape
    return pl.pallas_call(
        _final_norm_kernel,
        grid=(t // TB_NORM,),
        in_specs=[pl.BlockSpec((TB_NORM, d), lambda i: (i, 0)),
                  pl.BlockSpec((1, d), lambda i: (0, 0))],
        out_specs=pl.BlockSpec((TB_NORM, d), lambda i: (i, 0)),
        out_shape=jax.ShapeDtypeStruct((t, d), F32),
        compiler_params=_params("arbitrary"),
    )(x, w.reshape(1, d))


def _mm_kernel(a_ref, w_ref, o_ref, wb_ref):
    @pl.when(pl.program_id(1) == 0)
    def _():
        wb_ref[...] = w_ref[...].astype(BF16)
    o_ref[...] = _dot(a_ref[...], wb_ref[...]).astype(o_ref.dtype)


def _mm_relu2_kernel(a_ref, w_ref, o_ref, wb_ref):
    @pl.when(pl.program_id(1) == 0)
    def _():
        wb_ref[...] = w_ref[...].astype(BF16)
    y = jnp.maximum(_dot(a_ref[...], wb_ref[...]), 0.0)
    o_ref[...] = (y * y).astype(o_ref.dtype)


def _mm_residual_kernel(a_ref, w_ref, x_ref, g_ref, o_ref, wb_ref):
    @pl.when(pl.program_id(1) == 0)
    def _():
        wb_ref[...] = w_ref[...].astype(BF16)
    o_ref[...] = x_ref[...] + g_ref[...] * _dot(a_ref[...], wb_ref[...])


def _matmul(a, w, layer, *, tm=TM, out_dtype=F32, relu2=False):
    t, k = a.shape
    n = w.shape[2]
    tn = min(TN, n)
    tm = min(tm, t)
    return pl.pallas_call(
        _mm_relu2_kernel if relu2 else _mm_kernel,
        grid=(n // tn, t // tm),
        in_specs=[pl.BlockSpec((tm, k), lambda j, i: (i, 0)),
                  pl.BlockSpec((None, k, tn), lambda j, i: (layer, 0, j))],
        out_specs=pl.BlockSpec((tm, tn), lambda j, i: (i, j)),
        out_shape=jax.ShapeDtypeStruct((t, n), out_dtype),
        scratch_shapes=[pltpu.VMEM((k, tn), BF16)],
        compiler_params=_params("arbitrary", "arbitrary"),
    )(a, w)


def _matmul_residual(a, w, layer, x, gate):
    t, k = a.shape
    n = w.shape[2]
    tn = min(TN, n)
    return pl.pallas_call(
        _mm_residual_kernel,
        grid=(n // tn, t // TM),
        in_specs=[pl.BlockSpec((TM, k), lambda j, i: (i, 0)),
                  pl.BlockSpec((None, k, tn), lambda j, i: (layer, 0, j)),
                  pl.BlockSpec((TM, tn), lambda j, i: (i, j)),
                  pl.BlockSpec((1, tn), lambda j, i: (0, j))],
        out_specs=pl.BlockSpec((TM, tn), lambda j, i: (i, j)),
        out_shape=jax.ShapeDtypeStruct((t, n), F32),
        scratch_shapes=[pltpu.VMEM((k, tn), BF16)],
        compiler_params=_params("arbitrary", "arbitrary"),
    )(a, w, x, gate)


def _proj_kernel(*refs, mode, shifted):
    if mode in ("conv", "conv_l2"):
        a_ref, w_ref, cw_ref, o_ref, wb_ref, carry_ref = refs
    elif shifted:
        a_ref, w_ref, wn_ref, o_ref, wb_ref = refs
    else:
        a_ref, w_ref, o_ref, wb_ref = refs
    j = pl.program_id(0)
    tm, tn = o_ref.shape

    @pl.when(pl.program_id(1) == 0)
    def _():
        if shifted:
            keep = w_ref.shape[0] - BA_COLS
            wb_ref[0:keep, :] = w_ref[BA_COLS:, :].astype(BF16)
            wb_ref[keep:, :] = wn_ref[...].astype(BF16)
        else:
            wb_ref[...] = w_ref[...].astype(BF16)
        if mode in ("conv", "conv_l2"):
            carry_ref[...] = jnp.zeros_like(carry_ref)

    wb = wb_ref[...]
    hd = GDN_HEAD_DIM
    prev = carry_ref[...] if mode in ("conv", "conv_l2") else None
    for r in range(tm // ROW_CHUNK):
        rows = slice(r * ROW_CHUNK, (r + 1) * ROW_CHUNK)
        y = _dot_nt(a_ref[rows, :], wb)
        if mode == "plain":
            o_ref[rows, :] = y
        elif mode == "silu":
            o_ref[rows, :] = _silu(y)
        else:
            full = jnp.concatenate([prev, y], axis=0)
            acc = cw_ref[SHORT_CONV - 1:SHORT_CONV, :] * y
            for tap in range(SHORT_CONV - 1):
                acc = acc + cw_ref[tap:tap + 1, :] * pltpu.roll(full, SHORT_CONV - 1 - tap, axis=0)[SUBLANES:]
            prev = y[ROW_CHUNK - SUBLANES:, :]
            s = _silu(acc)
            if mode == "conv":
                o_ref[rows, :] = s
            else:
                scale = jnp.where(j < GDN_DIM // tn, hd ** -0.5, 1.0)
                for g in range(tn // hd):
                    blk = s[:, g * hd:(g + 1) * hd]
                    inv = lax.rsqrt(jnp.sum(blk * blk, axis=-1, keepdims=True) + L2_EPS) * scale
                    o_ref[rows, g * hd:(g + 1) * hd] = blk * inv
    if mode in ("conv", "conv_l2"):
        carry_ref[...] = prev


def _proj(a, w_in_t, layer, row0, width, mode, conv_w=None, conv_col0=0):
    t, k = a.shape
    tn = TN
    shifted = row0 >= P_RQ
    r0 = row0 - BA_COLS if shifted else row0
    assert r0 % tn == 0 and width % tn == 0
    in_specs = [pl.BlockSpec((TM, k), lambda j, i: (i, 0)),
                pl.BlockSpec((None, tn, k), lambda j, i: (layer, r0 // tn + j, 0))]
    args = [a, w_in_t]
    scratch = [pltpu.VMEM((tn, k), BF16)]
    if shifted:
        per = tn // BA_COLS
        in_specs.append(pl.BlockSpec((None, BA_COLS, k), lambda j, i: (layer, (r0 // tn + j + 1) * per, 0)))
        args.append(w_in_t)
    if mode in ("conv", "conv_l2"):
        in_specs.append(pl.BlockSpec((SHORT_CONV, tn), lambda j, i: (0, conv_col0 // tn + j)))
        args.append(conv_w)
        scratch.append(pltpu.VMEM((SUBLANES, tn), F32))
    return pl.pallas_call(
        functools.partial(_proj_kernel, mode=mode, shifted=shifted),
        grid=(width // tn, t // TM),
        in_specs=in_specs,
        out_specs=pl.BlockSpec((TM, tn), lambda j, i: (i, j)),
        out_shape=jax.ShapeDtypeStruct((t, width), F32),
        scratch_shapes=scratch,
        compiler_params=_params("arbitrary", "arbitrary"),
    )(*args)


def _mlp_out_kernel(a_ref, w_ref, x_ref, g_ref, o_ref, acc_ref):
    kk = pl.program_id(2)

    @pl.when(kk == 0)
    def _():
        acc_ref[...] = jnp.zeros_like(acc_ref)
    acc_ref[...] += _dot(a_ref[...], w_ref[...].astype(BF16))

    @pl.when(kk == pl.num_programs(2) - 1)
    def _():
        o_ref[...] = x_ref[...] + g_ref[...] * acc_ref[...]


def _mlp_out(a, w, layer, x, gate):
    t, k = a.shape
    n = w.shape[2]
    tm, tn, tk = 1024, 1024, 2048
    return pl.pallas_call(
        _mlp_out_kernel,
        grid=(t // tm, n // tn, k // tk),
        in_specs=[pl.BlockSpec((tm, tk), lambda i, j, kk: (i, kk)),
                  pl.BlockSpec((None, tk, tn), lambda i, j, kk: (layer, kk, j)),
                  pl.BlockSpec((tm, tn), lambda i, j, kk: (i, j)),
                  pl.BlockSpec((1, tn), lambda i, j, kk: (0, j))],
        out_specs=pl.BlockSpec((tm, tn), lambda i, j, kk: (i, j)),
        out_shape=jax.ShapeDtypeStruct((t, n), F32),
        scratch_shapes=[pltpu.VMEM((tm, tn), F32)],
        compiler_params=_params("arbitrary", "arbitrary", "arbitrary"),
    )(a, w, x, gate)


def _gdn_gate_kernel(h_ref, w_ref, alog_ref, dtb_ref, bg_ref, bgt_ref):
    tb = h_ref.shape[0]
    ba = _dot_nt(h_ref[...], w_ref[...].astype(BF16))
    beta = jax.nn.sigmoid(ba)
    xs = ba + dtb_ref[...]
    softplus = jnp.maximum(xs, 0.0) + jnp.log1p(jnp.exp(-jnp.abs(xs)))
    g = -jnp.exp(alog_ref[...]) * softplus
    ck = GDN_CHUNK
    ii = lax.broadcasted_iota(jnp.int32, (ck, ck), 0)
    jj = lax.broadcasted_iota(jnp.int32, (ck, ck), 1)
    tri = jnp.where(ii >= jj, 1.0, 0.0).astype(BF16)
    ones = jnp.ones((ck, ck), BF16)
    chunks = [g[s * ck:(s + 1) * ck] for s in range(tb // ck)]
    gc = jnp.concatenate([_dot_exact_lhs(tri, gs) for gs in chunks], axis=0)
    gl = jnp.concatenate([_dot_exact_lhs(ones, gs) for gs in chunks], axis=0)
    lane = lax.broadcasted_iota(jnp.int32, (tb, LANES), 1)
    out = jnp.where(lane < GDN_HEADS, beta,
                    jnp.where(lane < 2 * GDN_HEADS, gc,
                              jnp.where(lane < 3 * GDN_HEADS, pltpu.roll(gl, GDN_HEADS, axis=1), 0.0)))
    bg_ref[...] = out
    bgt_ref[...] = out.T


def _gdn_gates(h, w_in_t, layer, a_log, dt_bias):
    t, d = h.shape
    tb = TB_GATE
    lead = jnp.zeros((GDN_HEADS,), F32)
    tail = jnp.zeros((LANES - 2 * GDN_HEADS,), F32)
    alog = jnp.concatenate([lead, a_log, tail]).reshape(1, LANES)
    dtb = jnp.concatenate([lead, dt_bias, tail]).reshape(1, LANES)
    return pl.pallas_call(
        _gdn_gate_kernel,
        grid=(t // tb,),
        in_specs=[pl.BlockSpec((tb, d), lambda i: (i, 0)),
                  pl.BlockSpec((None, LANES, d), lambda i: (layer, P_RQ // LANES, 0)),
                  pl.BlockSpec((1, LANES), lambda i: (0, 0)),
                  pl.BlockSpec((1, LANES), lambda i: (0, 0))],
        out_specs=[pl.BlockSpec((tb, LANES), lambda i: (i, 0)),
                   pl.BlockSpec((LANES, tb), lambda i: (0, i))],
        out_shape=[jax.ShapeDtypeStruct((t, LANES), F32),
                   jax.ShapeDtypeStruct((LANES, t), F32)],
        compiler_params=_params("arbitrary"),
    )(h, w_in_t, alog, dtb)


def _tri_inverse_all(lows):
    n = lows[0].shape[0]
    ii = lax.broadcasted_iota(jnp.int32, (n, n), 0)
    jj = lax.broadcasted_iota(jnp.int32, (n, n), 1)
    ts = None
    for level in range(int(math.log2(n))):
        rb = jnp.right_shift(ii, level)
        cb = jnp.right_shift(jj, level)
        sel = (jnp.bitwise_and(rb, 1) == 1) & (cb == rb - 1)
        if level == 0:
            eye = jnp.where(ii == jj, 1.0, 0.0)
            ts = [eye - jnp.where(sel, low, 0.0) for low in lows]
        else:
            t16 = [t.astype(BF16) for t in ts]
            ys = [_dot(jnp.where(sel, low, 0.0).astype(BF16), t).astype(BF16) for low, t in zip(lows, t16)]
            ts = [t - _dot(tb16, y) for t, tb16, y in zip(ts, t16, ys)]
    return ts


def _gdn_kernel(q_ref, k_ref, v_ref, z_ref, bg_ref, bgt_ref, nw_ref, o_ref, s_ref):
    tb = q_ref.shape[0]
    hd = GDN_HEAD_DIM

    @pl.when(pl.program_id(0) == 0)
    def _():
        s_ref[...] = jnp.zeros_like(s_ref)

    bg = bg_ref[...]
    bgt = bgt_ref[...]
    ii = lax.broadcasted_iota(jnp.int32, (tb, tb), 0)
    jj = lax.broadcasted_iota(jnp.int32, (tb, tb), 1)
    causal = ii >= jj
    strict = ii > jj
    nw = nw_ref[...]
    heads = range(GDN_HEADS)
    sl = [slice(h * hd, (h + 1) * hd) for h in heads]

    q = [q_ref[:, sl[h]] for h in heads]
    k = [k_ref[:, sl[h]] for h in heads]
    beta = [bg[:, h:h + 1] for h in heads]
    gc = [bg[:, GDN_HEADS + h:GDN_HEADS + h + 1] for h in heads]
    gl = [bg[:, 2 * GDN_HEADS + h:2 * GDN_HEADS + h + 1] for h in heads]
    eg = [jnp.exp(gc[h]) for h in heads]
    decay = [jnp.exp(jnp.where(causal, gc[h] - bgt[h:h + 1, :], -jnp.inf)) for h in heads]
    kb = [k[h] * beta[h] for h in heads]
    kq = [_dot_nt(jnp.concatenate([kb[h], q[h]], axis=0).astype(BF16), k[h].astype(BF16)) for h in heads]
    low = [jnp.where(strict, kq[h][:tb] * decay[h], 0.0) for h in heads]
    attn = [(kq[h][tb:] * decay[h]).astype(BF16) for h in heads]
    t_inv = _tri_inverse_all(low)
    rhs = [jnp.concatenate([v_ref[:, sl[h]] * beta[h], kb[h] * eg[h]], axis=1).astype(BF16) for h in heads]
    uw = [_dot(t_inv[h].astype(BF16), rhs[h]) for h in heads]
    state = [s_ref[h] for h in heads]
    ws = [_dot(jnp.concatenate([uw[h][:, hd:], q[h] * eg[h]], axis=0).astype(BF16), state[h].astype(BF16))
          for h in heads]
    v_new = [(uw[h][:, :hd] - ws[h][:tb]).astype(BF16) for h in heads]
    k_dec_t = [(k[h] * jnp.exp(gl[h] - gc[h])).T.astype(BF16) for h in heads]
    for h in heads:
        egl = jnp.broadcast_to(jnp.exp(gl[h]), (tb, hd))[0:1, :]
        s_ref[h] = state[h] * egl + _dot(k_dec_t[h], v_new[h])
    o = [ws[h][tb:] + _dot(attn[h], v_new[h]) for h in heads]
    o = [o[h] * lax.rsqrt(jnp.mean(o[h] * o[h], axis=-1, keepdims=True) + NORM_EPS) * nw for h in heads]
    o_ref[...] = jnp.concatenate([(o[h] * z_ref[:, sl[h]]).astype(o_ref.dtype) for h in heads], axis=1)


def _gdn(qk, v, z, bg, bgt, norm_w):
    t = qk.shape[0]
    tb = GDN_CHUNK
    gd, hd = GDN_DIM, GDN_HEAD_DIM
    col = lambda c: pl.BlockSpec((tb, gd), lambda i: (i, c))
    return pl.pallas_call(
        _gdn_kernel,
        grid=(t // tb,),
        in_specs=[col(0), col(1), col(0), col(0),
                  pl.BlockSpec((tb, LANES), lambda i: (i, 0)),
                  pl.BlockSpec((SUBLANES, tb), lambda i: (1, i)),
                  pl.BlockSpec((1, hd), lambda i: (0, 0))],
        out_specs=pl.BlockSpec((tb, gd), lambda i: (i, 0)),
        out_shape=jax.ShapeDtypeStruct((t, gd), BF16),
        scratch_shapes=[pltpu.VMEM((GDN_HEADS, hd, hd), F32)],
        compiler_params=_params("arbitrary"),
    )(qk, qk, v, z, bg, bgt, norm_w.reshape(1, hd))


def _rope_kernel(pos_ref, cos_ref, sin_ref):
    half = RET_QK_DIM // 2
    lane = lax.broadcasted_iota(jnp.int32, (1, RET_QK_DIM), 1)
    idx = jnp.where(lane < half, lane, lane - half).astype(F32)
    inv_freq = jnp.exp(idx * (-math.log(ROPE_BASE) / half))
    ang = pos_ref[...].astype(F32) * inv_freq
    cos_ref[...] = jnp.cos(ang)
    sin_ref[...] = jnp.where(lane < half, -1.0, 1.0) * jnp.sin(ang)


def _rope_tables(positions):
    t = positions.shape[0]
    tb = 1024
    return pl.pallas_call(
        _rope_kernel,
        grid=(t // tb,),
        in_specs=[pl.BlockSpec((tb, 1), lambda i: (i, 0))],
        out_specs=[pl.BlockSpec((tb, RET_QK_DIM), lambda i: (i, 0))] * 2,
        out_shape=[jax.ShapeDtypeStruct((t, RET_QK_DIM), F32)] * 2,
        compiler_params=_params("arbitrary"),
    )(positions.reshape(t, 1))


def _ret_kernel(q_ref, k_ref, v_ref, g_ref, cos_ref, sin_ref, o_ref, r_ref, dmask_ref, qs_ref, ks_ref):
    tb = q_ref.shape[0]
    qk, dv = RET_QK_DIM, RET_V_DIM
    heads = range(RET_HEADS)
    log_gamma = [math.log(1.0 - 2.0 ** (-RET_DECAY_BASE - h)) for h in heads]

    @pl.when(pl.program_id(0) == 0)
    def _():
        r_ref[...] = jnp.zeros_like(r_ref)
        ii = lax.broadcasted_iota(jnp.int32, (tb, tb), 0)
        jj = lax.broadcasted_iota(jnp.int32, (tb, tb), 1)
        rel = (ii - jj).astype(F32)
        idx = lax.broadcasted_iota(jnp.int32, (tb, qk), 0).astype(F32)
        for h in heads:
            dmask_ref[h] = jnp.where(rel >= 0.0, jnp.exp(jnp.maximum(rel, 0.0) * log_gamma[h]), 0.0)
            qs_ref[h] = jnp.exp((idx + 1.0) * log_gamma[h])
            ks_ref[h] = jnp.exp((tb - 1.0 - idx) * log_gamma[h])

    cos, sin = cos_ref[...], sin_ref[...]
    rope = lambda x: x * cos + pltpu.roll(x, qk // 2, axis=1) * sin
    q = [rope(q_ref[:, h * qk:(h + 1) * qk]) for h in heads]
    k = [rope(k_ref[:, h * qk:(h + 1) * qk]) * (qk ** -0.5) for h in heads]
    v16 = [v_ref[:, h * dv:(h + 1) * dv].astype(BF16) for h in heads]
    scores = [(_dot_nt(q[h].astype(BF16), k[h].astype(BF16)) * dmask_ref[h]).astype(BF16) for h in heads]
    state = [r_ref[h] for h in heads]
    o = [_dot(scores[h], v16[h]) + _dot((q[h] * qs_ref[h]).astype(BF16), state[h].astype(BF16)) for h in heads]
    for h in heads:
        r_ref[h] = state[h] * math.exp(tb * log_gamma[h]) + _dot((k[h] * ks_ref[h]).T.astype(BF16), v16[h])
    outs = []
    for h in heads:
        oc = o[h] - jnp.mean(o[h], axis=-1, keepdims=True)
        on = oc * lax.rsqrt(jnp.mean(oc * oc, axis=-1, keepdims=True) + LN_EPS)
        outs.append((on * _silu(g_ref[:, h * dv:(h + 1) * dv])).astype(o_ref.dtype))
    o_ref[...] = jnp.concatenate(outs, axis=1)


def _retention(rest, cos, sin):
    t = rest.shape[0]
    tb = TB_RET
    qk, dv, nh = RET_QK_DIM, RET_V_DIM, RET_HEADS
    return pl.pallas_call(
        _ret_kernel,
        grid=(t // tb,),
        in_specs=[pl.BlockSpec((tb, RET_QK), lambda i: (i, R_RQ // RET_QK)),
                  pl.BlockSpec((tb, RET_QK), lambda i: (i, R_RK // RET_QK)),
                  pl.BlockSpec((tb, RET_V), lambda i: (i, R_RV // RET_V)),
                  pl.BlockSpec((tb, RET_V), lambda i: (i, R_RG // RET_V)),
                  pl.BlockSpec((tb, qk), lambda i: (i, 0)),
                  pl.BlockSpec((tb, qk), lambda i: (i, 0))],
        out_specs=pl.BlockSpec((tb, RET_V), lambda i: (i, 0)),
        out_shape=jax.ShapeDtypeStruct((t, RET_V), BF16),
        scratch_shapes=[pltpu.VMEM((nh, qk, dv), F32),
                        pltpu.VMEM((nh, tb, tb), F32),
                        pltpu.VMEM((nh, tb, qk), F32),
                        pltpu.VMEM((nh, tb, qk), F32)],
        compiler_params=_params("arbitrary"),
    )(rest, rest, rest, rest, cos, sin)


def _conformer_kernel(ca_ref, cb_ref, w_ref, b_ref, lnw_ref, lnb_ref, o_ref, upad_ref, sh_ref):
    tb = ca_ref.shape[0]
    rows = tb + HALO

    @pl.when(pl.program_id(0) == 0)
    def _():
        upad_ref[0:HALO, :] = jnp.zeros((HALO, CONV_DIM), F32)

    upad_ref[HALO:, :] = ca_ref[...] * jax.nn.sigmoid(cb_ref[...])
    full = upad_ref[...]
    for r in range(1, SUBLANES):
        sh_ref[r - 1] = pltpu.roll(full, rows - r, axis=0)
    base = HALO - (CONV_WIDTH - 1)
    acc = None
    for j in range(CONV_WIDTH):
        off = base + j
        r, a = off % SUBLANES, off - off % SUBLANES
        src = upad_ref[a:a + tb, :] if r == 0 else sh_ref[r - 1, a:a + tb, :]
        term = w_ref[j:j + 1, :] * src
        acc = term if acc is None else acc + term
    upad_ref[0:HALO, :] = upad_ref[tb:tb + HALO, :]
    u = acc + b_ref[...]
    mu = jnp.mean(u, axis=-1, keepdims=True)
    uc = u - mu
    u = uc * lax.rsqrt(jnp.mean(uc * uc, axis=-1, keepdims=True) + LN_EPS) * lnw_ref[...] + lnb_ref[...]
    o_ref[...] = _silu(u).astype(o_ref.dtype)


def _conformer(rest, w, b, ln_w, ln_b):
    t = rest.shape[0]
    tb = TB_CONV
    c = CONV_DIM
    vec = lambda: pl.BlockSpec((1, c), lambda i: (0, 0))
    return pl.pallas_call(
        _conformer_kernel,
        grid=(t // tb,),
        in_specs=[pl.BlockSpec((tb, c), lambda i: (i, R_GLU // c)),
                  pl.BlockSpec((tb, c), lambda i: (i, R_GLU // c + 1)),
                  pl.BlockSpec((CONV_WIDTH, c), lambda i: (0, 0)),
                  vec(), vec(), vec()],
        out_specs=pl.BlockSpec((tb, c), lambda i: (i, 0)),
        out_shape=jax.ShapeDtypeStruct((t, c), BF16),
        scratch_shapes=[pltpu.VMEM((tb + HALO, c), F32),
                        pltpu.VMEM((SUBLANES - 1, tb + HALO, c), F32)],
        compiler_params=_params("arbitrary"),
    )(rest, rest, w, b.reshape(1, c), ln_w.reshape(1, c), ln_b.reshape(1, c))


def _merge_kernel(oa_ref, ob_ref, oc_ref, wa_ref, wb_ref, wc_ref, ga_ref, gb_ref, gc_ref, o_ref,
                  wa16_ref, wb16_ref, wc16_ref):
    @pl.when(pl.program_id(1) == 0)
    def _():
        wa16_ref[...] = wa_ref[...].astype(BF16)
        wb16_ref[...] = wb_ref[...].astype(BF16)
        wc16_ref[...] = wc_ref[...].astype(BF16)
    m = jax.nn.sigmoid(ga_ref[...]) * _dot(oa_ref[...], wa16_ref[...])
    m = m + jax.nn.sigmoid(gb_ref[...]) * _dot(ob_ref[...], wb16_ref[...])
    m = m + jax.nn.sigmoid(gc_ref[...]) * _dot(oc_ref[...], wc16_ref[...])
    o_ref[...] = m.astype(o_ref.dtype)


def _merge(o_a, o_b, o_c, w_a, w_b, w_c, layer, rest):
    t, k = o_a.shape
    n = w_a.shape[2]
    tm, tn = TM_MERGE, TN
    act = lambda: pl.BlockSpec((tm, k), lambda j, i: (i, 0))
    wsp = lambda: pl.BlockSpec((None, k, tn), lambda j, i: (layer, 0, j))
    gate = lambda b: pl.BlockSpec((tm, tn), lambda j, i: (i, (R_GATE + b * n) // tn + j))
    return pl.pallas_call(
        _merge_kernel,
        grid=(n // tn, t // tm),
        in_specs=[act(), act(), act(), wsp(), wsp(), wsp(), gate(0), gate(1), gate(2)],
        out_specs=pl.BlockSpec((tm, tn), lambda j, i: (i, j)),
        out_shape=jax.ShapeDtypeStruct((t, n), BF16),
        scratch_shapes=[pltpu.VMEM((k, tn), BF16)] * 3,
        compiler_params=_params("arbitrary", "arbitrary"),
    )(o_a, o_b, o_c, w_a, w_b, w_c, rest, rest, rest)


def kernel(x, c, positions, w_ada, b_ada, norm_mix_w, norm_mlp_w, w_in, conv_qkv_w, gdn_a_log, gdn_dt_bias,
           gdn_norm_w, conv_dw_w, conv_dw_b, conv_ln_w, conv_ln_b, w_branch_a, w_branch_b, w_branch_c,
           w_out, w_mlp_in, w_mlp_out, final_norm_w):
    bsz, t, d = x.shape
    assert bsz == 1
    xs = x.reshape(t, d)
    mod = _adaln_mod(c, w_ada, b_ada)
    cos, sin = _rope_tables(positions.reshape(t))
    w_in_t = jnp.swapaxes(w_in, 1, 2)
    for l in range(w_in.shape[0]):
        h = _norm_mod(xs, norm_mix_w[l], mod[l], 0, 1)
        qk = _proj(h, w_in_t, l, 0, 2 * GDN_DIM, "conv_l2", conv_qkv_w[l], 0)
        v = _proj(h, w_in_t, l, 2 * GDN_DIM, GDN_DIM, "conv", conv_qkv_w[l], 2 * GDN_DIM)
        z = _proj(h, w_in_t, l, 3 * GDN_DIM, GDN_DIM, "silu")
        rest = _proj(h, w_in_t, l, P_RQ + BA_COLS, R_WIDTH, "plain")
        bg, bgt = _gdn_gates(h, w_in_t, l, gdn_a_log[l], gdn_dt_bias[l])
        o_a = _gdn(qk, v, z, bg, bgt, gdn_norm_w[l])
        o_b = _retention(rest, cos, sin)
        o_c = _conformer(rest, conv_dw_w[l], conv_dw_b[l], conv_ln_w[l], conv_ln_b[l])
        merged = _merge(o_a, o_b, o_c, w_branch_a, w_branch_b, w_branch_c, l, rest)
        xs = _matmul_residual(merged, w_out, l, xs, mod[l, 2:3])
        h = _norm_mod(xs, norm_mlp_w[l], mod[l], 3, 4)
        act = _matmul(h, w_mlp_in, l, tm=TM_WIDE, out_dtype=BF16, relu2=True)
        xs = _mlp_out(act, w_mlp_out, l, xs, mod[l, 5:6])
    return _final_norm(xs, final_norm_w).reshape(bsz, t, d)
```

```python
import functools
import math

import jax
import jax.numpy as jnp
from jax import lax
from jax.experimental import pallas as pl
from jax.experimental.pallas import tpu as pltpu

F32 = jnp.float32
BF16 = jnp.bfloat16

D_MODEL = 2048
GDN_HEADS = 8
GDN_HEAD_DIM = 128
GDN_DIM = GDN_HEADS * GDN_HEAD_DIM
SHORT_CONV = 4
RET_HEADS = 4
RET_QK_DIM = 128
RET_V_DIM = 256
RET_QK = RET_HEADS * RET_QK_DIM
RET_V = RET_HEADS * RET_V_DIM
RET_DECAY_BASE = 5.0
ROPE_BASE = 10000.0
CONV_DIM = D_MODEL // 2
CONV_WIDTH = 31
N_BRANCH = 3
NORM_EPS = 1e-6
LN_EPS = 1e-5
L2_EPS = 1e-6

BA_COLS = 2 * GDN_HEADS
P_RQ = 4 * GDN_DIM
R_RQ = 0
R_RK = R_RQ + RET_QK
R_RV = R_RK + RET_QK
R_RG = R_RV + RET_V
R_GLU = R_RG + RET_V
R_GATE = R_GLU + 2 * CONV_DIM
R_WIDTH = R_GATE + N_BRANCH * D_MODEL

LANES = 128
SUBLANES = 8
VMEM_LIMIT_BYTES = 60 * 1024 * 1024
TM = 1024
TM_WIDE = 2048
TM_PROJ = 2048
TM_MERGE = 512
TN = 1024
ROW_CHUNK = 256
GDN_CHUNK = 128
TB_GATE = 512
TB_RET = 256
TB_CONV = 256
TB_NORM = 1024
HALO = 32


def _params(*sem):
    return pltpu.CompilerParams(dimension_semantics=sem, vmem_limit_bytes=VMEM_LIMIT_BYTES)


def _dot(a, b):
    return jnp.dot(a, b, preferred_element_type=F32)


def _dot_nt(a, b):
    return lax.dot_general(a, b, (((1,), (1,)), ((), ())), preferred_element_type=F32)


def _dot_exact_lhs(m_bf16, g):
    g1 = g.astype(BF16)
    r1 = g - g1.astype(F32)
    g2 = r1.astype(BF16)
    g3 = (r1 - g2.astype(F32)).astype(BF16)
    return _dot(m_bf16, g1) + _dot(m_bf16, g2) + _dot(m_bf16, g3)


def _silu(x):
    return x * jax.nn.sigmoid(x)


def _mod_kernel(c_ref, w_ref, b_ref, o_ref):
    prod = _silu(c_ref[...]) * w_ref[0]
    d, tn = prod.shape
    part = jnp.sum(prod.reshape(d // SUBLANES, SUBLANES, tn), axis=0)
    o_ref[0] = jnp.sum(part, axis=0, keepdims=True) + b_ref[0]


def _adaln_mod(c, w_ada, b_ada):
    depth, d, n = w_ada.shape
    tn = 1024
    out = pl.pallas_call(
        _mod_kernel,
        grid=(depth, n // tn),
        in_specs=[pl.BlockSpec((d, 1), lambda l, j: (0, 0)),
                  pl.BlockSpec((1, d, tn), lambda l, j: (l, 0, j)),
                  pl.BlockSpec((1, 1, tn), lambda l, j: (l, 0, j))],
        out_specs=pl.BlockSpec((1, 1, tn), lambda l, j: (l, 0, j)),
        out_shape=jax.ShapeDtypeStruct((depth, 1, n), F32),
        compiler_params=_params("arbitrary", "arbitrary"),
    )(c.reshape(d, 1), w_ada, b_ada.reshape(depth, 1, n))
    return out.reshape(depth, 6, d)


def _norm_mod_kernel(x_ref, w_ref, mod_ref, o_ref, *, shift_row, scale_row):
    x = x_ref[...]
    y = x * lax.rsqrt(jnp.mean(x * x, axis=-1, keepdims=True) + NORM_EPS) * w_ref[...]
    y = y * (1.0 + mod_ref[scale_row:scale_row + 1, :]) + mod_ref[shift_row:shift_row + 1, :]
    o_ref[...] = y.astype(o_ref.dtype)


def _norm_mod(x, w, mod, shift_row, scale_row):
    t, d = x.shape
    return pl.pallas_call(
        functools.partial(_norm_mod_kernel, shift_row=shift_row, scale_row=scale_row),
        grid=(t // TB_NORM,),
        in_specs=[pl.BlockSpec((TB_NORM, d), lambda i: (i, 0)),
                  pl.BlockSpec((1, d), lambda i: (0, 0)),
                  pl.BlockSpec((6, d), lambda i: (0, 0))],
        out_specs=pl.BlockSpec((TB_NORM, d), lambda i: (i, 0)),
        out_shape=jax.ShapeDtypeStruct((t, d), BF16),
        compiler_params=_params("arbitrary"),
    )(x, w.reshape(1, d), mod)


def _final_norm_kernel(x_ref, w_ref, o_ref):
    x = x_ref[...]
    o_ref[...] = x * lax.rsqrt(jnp.mean(x * x, axis=-1, keepdims=True) + NORM_EPS) * w_ref[...]


def _final_norm(x, w):
    t, d = x.shape
    return pl.pallas_call(
        _final_norm_kernel,
        grid=(t // TB_NORM,),
        in_specs=[pl.BlockSpec((TB_NORM, d), lambda i: (i, 0)),
                  pl.BlockSpec((1, d), lambda i: (0, 0))],
        out_specs=pl.BlockSpec((TB_NORM, d), lambda i: (i, 0)),
        out_shape=jax.ShapeDtypeStruct((t, d), F32),
        compiler_params=_params("arbitrary"),
    )(x, w.reshape(1, d))


def _mm_kernel(a_ref, w_ref, o_ref, wb_ref):
    @pl.when(pl.program_id(1) == 0)
    def _():
        wb_ref[...] = w_ref[...].astype(BF16)
    o_ref[...] = _dot(a_ref[...], wb_ref[...]).astype(o_ref.dtype)


def _mm_relu2_kernel(a_ref, w_ref, o_ref, wb_ref):
    @pl.when(pl.program_id(1) == 0)
    def _():
        wb_ref[...] = w_ref[...].astype(BF16)
    y = jnp.maximum(_dot(a_ref[...], wb_ref[...]), 0.0)
    o_ref[...] = (y * y).astype(o_ref.dtype)


def _mm_residual_kernel(a_ref, w_ref, x_ref, g_ref, o_ref, wb_ref):
    @pl.when(pl.program_id(1) == 0)
    def _():
        wb_ref[...] = w_ref[...].astype(BF16)
    o_ref[...] = x_ref[...] + g_ref[...] * _dot(a_ref[...], wb_ref[...])


def _matmul(a, w, layer, *, tm=TM, out_dtype=F32, relu2=False):
    t, k = a.shape
    n = w.shape[2]
    tn = min(TN, n)
    tm = min(tm, t)
    return pl.pallas_call(
        _mm_relu2_kernel if relu2 else _mm_kernel,
        grid=(n // tn, t // tm),
        in_specs=[pl.BlockSpec((tm, k), lambda j, i: (i, 0)),
                  pl.BlockSpec((None, k, tn), lambda j, i: (layer, 0, j))],
        out_specs=pl.BlockSpec((tm, tn), lambda j, i: (i, j)),
        out_shape=jax.ShapeDtypeStruct((t, n), out_dtype),
        scratch_shapes=[pltpu.VMEM((k, tn), BF16)],
        compiler_params=_params("arbitrary", "arbitrary"),
    )(a, w)


def _matmul_residual(a, w, layer, x, gate):
    t, k = a.shape
    n = w.shape[2]
    tn = min(TN, n)
    return pl.pallas_call(
        _mm_residual_kernel,
        grid=(n // tn, t // TM),
        in_specs=[pl.BlockSpec((TM, k), lambda j, i: (i, 0)),
                  pl.BlockSpec((None, k, tn), lambda j, i: (layer, 0, j)),
                  pl.BlockSpec((TM, tn), lambda j, i: (i, j)),
                  pl.BlockSpec((1, tn), lambda j, i: (0, j))],
        out_specs=pl.BlockSpec((TM, tn), lambda j, i: (i, j)),
        out_shape=jax.ShapeDtypeStruct((t, n), F32),
        scratch_shapes=[pltpu.VMEM((k, tn), BF16)],
        compiler_params=_params("arbitrary", "arbitrary"),
    )(a, w, x, gate)


def _proj_kernel(*refs, mode, shifted):
    if mode in ("conv", "conv_l2"):
        a_ref, w_ref, cw_ref, o_ref, wb_ref, carry_ref = refs
    elif shifted:
        a_ref, w_ref, wn_ref, o_ref, wb_ref = refs
    else:
        a_ref, w_ref, o_ref, wb_ref = refs
    j = pl.program_id(0)
    tm, tn = o_ref.shape

    @pl.when(pl.program_id(1) == 0)
    def _():
        if shifted:
            keep = w_ref.shape[0] - BA_COLS
            wb_ref[0:keep, :] = w_ref[BA_COLS:, :].astype(BF16)
            wb_ref[keep:, :] = wn_ref[...].astype(BF16)
        else:
            wb_ref[...] = w_ref[...].astype(BF16)
        if mode in ("conv", "conv_l2"):
            carry_ref[...] = jnp.zeros_like(carry_ref)

    wb = wb_ref[...]
    hd = GDN_HEAD_DIM
    prev = carry_ref[...] if mode in ("conv", "conv_l2") else None
    for r in range(tm // ROW_CHUNK):
        rows = slice(r * ROW_CHUNK, (r + 1) * ROW_CHUNK)
        y = _dot_nt(a_ref[rows, :], wb)
        if mode == "plain":
            o_ref[rows, :] = y
        elif mode == "silu":
            o_ref[rows, :] = _silu(y)
        elif mode == "sigmoid":
            o_ref[rows, :] = jax.nn.sigmoid(y).astype(o_ref.dtype)
        else:
            full = jnp.concatenate([prev, y], axis=0)
            acc = cw_ref[SHORT_CONV - 1:SHORT_CONV, :] * y
            for tap in range(SHORT_CONV - 1):
                acc = acc + cw_ref[tap:tap + 1, :] * pltpu.roll(full, SHORT_CONV - 1 - tap, axis=0)[SUBLANES:]
            prev = y[ROW_CHUNK - SUBLANES:, :]
            s = _silu(acc)
            if mode == "conv":
                o_ref[rows, :] = s
            else:
                scale = jnp.where(j < GDN_DIM // tn, hd ** -0.5, 1.0)
                for g in range(tn // hd):
                    blk = s[:, g * hd:(g + 1) * hd]
                    inv = lax.rsqrt(jnp.sum(blk * blk, axis=-1, keepdims=True) + L2_EPS) * scale
                    o_ref[rows, g * hd:(g + 1) * hd] = blk * inv
    if mode in ("conv", "conv_l2"):
        carry_ref[...] = prev


def _proj(a, w_in_t, layer, row0, width, mode, conv_w=None, conv_col0=0, out_dtype=F32):
    t, k = a.shape
    tn = TN
    shifted = row0 >= P_RQ
    r0 = row0 - BA_COLS if shifted else row0
    assert r0 % tn == 0 and width % tn == 0
    tm = min(TM if mode in ("conv", "conv_l2") else TM_PROJ, t)
    in_specs = [pl.BlockSpec((tm, k), lambda j, i: (i, 0)),
                pl.BlockSpec((None, tn, k), lambda j, i: (layer, r0 // tn + j, 0))]
    args = [a, w_in_t]
    scratch = [pltpu.VMEM((tn, k), BF16)]
    if shifted:
        per = tn // BA_COLS
        in_specs.append(pl.BlockSpec((None, BA_COLS, k), lambda j, i: (layer, (r0 // tn + j + 1) * per, 0)))
        args.append(w_in_t)
    if mode in ("conv", "conv_l2"):
        in_specs.append(pl.BlockSpec((SHORT_CONV, tn), lambda j, i: (0, conv_col0 // tn + j)))
        args.append(conv_w)
        scratch.append(pltpu.VMEM((SUBLANES, tn), F32))
    return pl.pallas_call(
        functools.partial(_proj_kernel, mode=mode, shifted=shifted),
        grid=(width // tn, t // tm),
        in_specs=in_specs,
        out_specs=pl.BlockSpec((tm, tn), lambda j, i: (i, j)),
        out_shape=jax.ShapeDtypeStruct((t, width), out_dtype),
        scratch_shapes=scratch,
        compiler_params=_params("arbitrary", "arbitrary"),
    )(*args)


def _mlp_out_kernel(a_ref, w_ref, x_ref, g_ref, o_ref, acc_ref):
    kk = pl.program_id(2)

    @pl.when(kk == 0)
    def _():
        acc_ref[...] = jnp.zeros_like(acc_ref)
    acc_ref[...] += _dot(a_ref[...], w_ref[...].astype(BF16))

    @pl.when(kk == pl.num_programs(2) - 1)
    def _():
        o_ref[...] = x_ref[...] + g_ref[...] * acc_ref[...]


def _mlp_out(a, w, layer, x, gate):
    t, k = a.shape
    n = w.shape[2]
    tm, tn, tk = 1024, 1024, 2048
    return pl.pallas_call(
        _mlp_out_kernel,
        grid=(t // tm, n // tn, k // tk),
        in_specs=[pl.BlockSpec((tm, tk), lambda i, j, kk: (i, kk)),
                  pl.BlockSpec((None, tk, tn), lambda i, j, kk: (layer, kk, j)),
                  pl.BlockSpec((tm, tn), lambda i, j, kk: (i, j)),
                  pl.BlockSpec((1, tn), lambda i, j, kk: (0, j))],
        out_specs=pl.BlockSpec((tm, tn), lambda i, j, kk: (i, j)),
        out_shape=jax.ShapeDtypeStruct((t, n), F32),
        scratch_shapes=[pltpu.VMEM((tm, tn), F32)],
        compiler_params=_params("arbitrary", "arbitrary", "arbitrary"),
    )(a, w, x, gate)


def _gdn_gate_kernel(h_ref, w_ref, alog_ref, dtb_ref, bg_ref, bgt_ref):
    tb = h_ref.shape[0]
    ba = _dot_nt(h_ref[...], w_ref[...].astype(BF16))
    beta = jax.nn.sigmoid(ba)
    xs = ba + dtb_ref[...]
    softplus = jnp.maximum(xs, 0.0) + jnp.log1p(jnp.exp(-jnp.abs(xs)))
    g = -jnp.exp(alog_ref[...]) * softplus
    ck = GDN_CHUNK
    ii = lax.broadcasted_iota(jnp.int32, (ck, ck), 0)
    jj = lax.broadcasted_iota(jnp.int32, (ck, ck), 1)
    tri = jnp.where(ii >= jj, 1.0, 0.0).astype(BF16)
    ones = jnp.ones((ck, ck), BF16)
    chunks = [g[s * ck:(s + 1) * ck] for s in range(tb // ck)]
    gc = jnp.concatenate([_dot_exact_lhs(tri, gs) for gs in chunks], axis=0)
    gl = jnp.concatenate([_dot_exact_lhs(ones, gs) for gs in chunks], axis=0)
    lane = lax.broadcasted_iota(jnp.int32, (tb, LANES), 1)
    out = jnp.where(lane < GDN_HEADS, beta,
                    jnp.where(lane < 2 * GDN_HEADS, gc,
                              jnp.where(lane < 3 * GDN_HEADS, pltpu.roll(gl, GDN_HEADS, axis=1), 0.0)))
    bg_ref[...] = out
    bgt_ref[...] = out.T


def _gdn_gates(h, w_in_t, layer, a_log, dt_bias):
    t, d = h.shape
    tb = TB_GATE
    lead = jnp.zeros((GDN_HEADS,), F32)
    tail = jnp.zeros((LANES - 2 * GDN_HEADS,), F32)
    alog = jnp.concatenate([lead, a_log, tail]).reshape(1, LANES)
    dtb = jnp.concatenate([lead, dt_bias, tail]).reshape(1, LANES)
    return pl.pallas_call(
        _gdn_gate_kernel,
        grid=(t // tb,),
        in_specs=[pl.BlockSpec((tb, d), lambda i: (i, 0)),
                  pl.BlockSpec((None, LANES, d), lambda i: (layer, P_RQ // LANES, 0)),
                  pl.BlockSpec((1, LANES), lambda i: (0, 0)),
                  pl.BlockSpec((1, LANES), lambda i: (0, 0))],
        out_specs=[pl.BlockSpec((tb, LANES), lambda i: (i, 0)),
                   pl.BlockSpec((LANES, tb), lambda i: (0, i))],
        out_shape=[jax.ShapeDtypeStruct((t, LANES), F32),
                   jax.ShapeDtypeStruct((LANES, t), F32)],
        compiler_params=_params("arbitrary"),
    )(h, w_in_t, alog, dtb)


def _tri_inverse_all(lows):
    n = lows[0].shape[0]
    ii = lax.broadcasted_iota(jnp.int32, (n, n), 0)
    jj = lax.broadcasted_iota(jnp.int32, (n, n), 1)
    ts = None
    for level in range(int(math.log2(n))):
        rb = jnp.right_shift(ii, level)
        cb = jnp.right_shift(jj, level)
        sel = (jnp.bitwise_and(rb, 1) == 1) & (cb == rb - 1)
        if level == 0:
            eye = jnp.where(ii == jj, 1.0, 0.0)
            ts = [eye - jnp.where(sel, low, 0.0) for low in lows]
        else:
            t16 = [t.astype(BF16) for t in ts]
            ys = [_dot(jnp.where(sel, low, 0.0).astype(BF16), t).astype(BF16) for low, t in zip(lows, t16)]
            ts = [t - _dot(tb16, y) for t, tb16, y in zip(ts, t16, ys)]
    return ts


def _gdn_kernel(q_ref, k_ref, v_ref, z_ref, bg_ref, bgt_ref, nw_ref, o_ref, s_ref):
    tb = q_ref.shape[0]
    hd = GDN_HEAD_DIM

    @pl.when(pl.program_id(0) == 0)
    def _():
        s_ref[...] = jnp.zeros_like(s_ref)

    bg = bg_ref[...]
    bgt = bgt_ref[...]
    ii = lax.broadcasted_iota(jnp.int32, (tb, tb), 0)
    jj = lax.broadcasted_iota(jnp.int32, (tb, tb), 1)
    causal = ii >= jj
    strict = ii > jj
    nw = nw_ref[...]
    heads = range(GDN_HEADS)
    sl = [slice(h * hd, (h + 1) * hd) for h in heads]

    q = [q_ref[:, sl[h]] for h in heads]
    k = [k_ref[:, sl[h]] for h in heads]
    beta = [bg[:, h:h + 1] for h in heads]
    gc = [bg[:, GDN_HEADS + h:GDN_HEADS + h + 1] for h in heads]
    gl = [bg[:, 2 * GDN_HEADS + h:2 * GDN_HEADS + h + 1] for h in heads]
    eg = [jnp.exp(gc[h]) for h in heads]
    decay = [jnp.exp(jnp.where(causal, gc[h] - bgt[h:h + 1, :], -jnp.inf)) for h in heads]
    kb = [k[h] * beta[h] for h in heads]
    kq = [_dot_nt(jnp.concatenate([kb[h], q[h]], axis=0).astype(BF16), k[h].astype(BF16)) for h in heads]
    low = [jnp.where(strict, kq[h][:tb] * decay[h], 0.0) for h in heads]
    attn = [(kq[h][tb:] * decay[h]).astype(BF16) for h in heads]
    t_inv = _tri_inverse_all(low)
    rhs = [jnp.concatenate([v_ref[:, sl[h]] * beta[h], kb[h] * eg[h]], axis=1).astype(BF16) for h in heads]
    uw = [_dot(t_inv[h].astype(BF16), rhs[h]) for h in heads]
    state = [s_ref[h] for h in heads]
    ws = [_dot(jnp.concatenate([uw[h][:, hd:], q[h] * eg[h]], axis=0).astype(BF16), state[h].astype(BF16))
          for h in heads]
    v_new = [(uw[h][:, :hd] - ws[h][:tb]).astype(BF16) for h in heads]
    k_dec_t = [(k[h] * jnp.exp(gl[h] - gc[h])).T.astype(BF16) for h in heads]
    for h in heads:
        egl = jnp.broadcast_to(jnp.exp(gl[h]), (tb, hd))[0:1, :]
        s_ref[h] = state[h] * egl + _dot(k_dec_t[h], v_new[h])
    o = [ws[h][tb:] + _dot(attn[h], v_new[h]) for h in heads]
    o = [o[h] * lax.rsqrt(jnp.mean(o[h] * o[h], axis=-1, keepdims=True) + NORM_EPS) * nw for h in heads]
    o_ref[...] = jnp.concatenate([(o[h] * z_ref[:, sl[h]]).astype(o_ref.dtype) for h in heads], axis=1)


def _gdn(qk, v, z, bg, bgt, norm_w):
    t = qk.shape[0]
    tb = GDN_CHUNK
    gd, hd = GDN_DIM, GDN_HEAD_DIM
    col = lambda c: pl.BlockSpec((tb, gd), lambda i: (i, c))
    return pl.pallas_call(
        _gdn_kernel,
        grid=(t // tb,),
        in_specs=[col(0), col(1), col(0), col(0),
                  pl.BlockSpec((tb, LANES), lambda i: (i, 0)),
                  pl.BlockSpec((SUBLANES, tb), lambda i: (1, i)),
                  pl.BlockSpec((1, hd), lambda i: (0, 0))],
        out_specs=pl.BlockSpec((tb, gd), lambda i: (i, 0)),
        out_shape=jax.ShapeDtypeStruct((t, gd), BF16),
        scratch_shapes=[pltpu.VMEM((GDN_HEADS, hd, hd), F32)],
        compiler_params=_params("arbitrary"),
    )(qk, qk, v, z, bg, bgt, norm_w.reshape(1, hd))


def _rope_kernel(pos_ref, cos_ref, sin_ref):
    half = RET_QK_DIM // 2
    lane = lax.broadcasted_iota(jnp.int32, (1, RET_QK_DIM), 1)
    idx = jnp.where(lane < half, lane, lane - half).astype(F32)
    inv_freq = jnp.exp(idx * (-math.log(ROPE_BASE) / half))
    ang = pos_ref[...].astype(F32) * inv_freq
    cos_ref[...] = jnp.cos(ang)
    sin_ref[...] = jnp.where(lane < half, -1.0, 1.0) * jnp.sin(ang)


def _rope_tables(positions):
    t = positions.shape[0]
    tb = 1024
    return pl.pallas_call(
        _rope_kernel,
        grid=(t // tb,),
        in_specs=[pl.BlockSpec((tb, 1), lambda i: (i, 0))],
        out_specs=[pl.BlockSpec((tb, RET_QK_DIM), lambda i: (i, 0))] * 2,
        out_shape=[jax.ShapeDtypeStruct((t, RET_QK_DIM), F32)] * 2,
        compiler_params=_params("arbitrary"),
    )(positions.reshape(t, 1))


def _ret_kernel(q_ref, k_ref, v_ref, g_ref, cos_ref, sin_ref, o_ref, r_ref, dmask_ref, qs_ref, ks_ref):
    tb = q_ref.shape[0]
    qk, dv = RET_QK_DIM, RET_V_DIM
    heads = range(RET_HEADS)
    log_gamma = [math.log(1.0 - 2.0 ** (-RET_DECAY_BASE - h)) for h in heads]

    @pl.when(pl.program_id(0) == 0)
    def _():
        r_ref[...] = jnp.zeros_like(r_ref)
        ii = lax.broadcasted_iota(jnp.int32, (tb, tb), 0)
        jj = lax.broadcasted_iota(jnp.int32, (tb, tb), 1)
        rel = (ii - jj).astype(F32)
        idx = lax.broadcasted_iota(jnp.int32, (tb, qk), 0).astype(F32)
        for h in heads:
            dmask_ref[h] = jnp.where(rel >= 0.0, jnp.exp(jnp.maximum(rel, 0.0) * log_gamma[h]), 0.0)
            qs_ref[h] = jnp.exp((idx + 1.0) * log_gamma[h])
            ks_ref[h] = jnp.exp((tb - 1.0 - idx) * log_gamma[h])

    cos, sin = cos_ref[...], sin_ref[...]
    rope = lambda x: x * cos + pltpu.roll(x, qk // 2, axis=1) * sin
    q = [rope(q_ref[:, h * qk:(h + 1) * qk]) for h in heads]
    k = [rope(k_ref[:, h * qk:(h + 1) * qk]) * (qk ** -0.5) for h in heads]
    v16 = [v_ref[:, h * dv:(h + 1) * dv].astype(BF16) for h in heads]
    scores = [(_dot_nt(q[h].astype(BF16), k[h].astype(BF16)) * dmask_ref[h]).astype(BF16) for h in heads]
    state = [r_ref[h] for h in heads]
    o = [_dot(scores[h], v16[h]) + _dot((q[h] * qs_ref[h]).astype(BF16), state[h].astype(BF16)) for h in heads]
    for h in heads:
        r_ref[h] = state[h] * math.exp(tb * log_gamma[h]) + _dot((k[h] * ks_ref[h]).T.astype(BF16), v16[h])
    outs = []
    for h in heads:
        oc = o[h] - jnp.mean(o[h], axis=-1, keepdims=True)
        on = oc * lax.rsqrt(jnp.mean(oc * oc, axis=-1, keepdims=True) + LN_EPS)
        outs.append((on * _silu(g_ref[:, h * dv:(h + 1) * dv])).astype(o_ref.dtype))
    o_ref[...] = jnp.concatenate(outs, axis=1)


def _retention(rest, cos, sin):
    t = rest.shape[0]
    tb = TB_RET
    qk, dv, nh = RET_QK_DIM, RET_V_DIM, RET_HEADS
    return pl.pallas_call(
        _ret_kernel,
        grid=(t // tb,),
        in_specs=[pl.BlockSpec((tb, RET_QK), lambda i: (i, R_RQ // RET_QK)),
                  pl.BlockSpec((tb, RET_QK), lambda i: (i, R_RK // RET_QK)),
                  pl.BlockSpec((tb, RET_V), lambda i: (i, R_RV // RET_V)),
                  pl.BlockSpec((tb, RET_V), lambda i: (i, R_RG // RET_V)),
                  pl.BlockSpec((tb, qk), lambda i: (i, 0)),
                  pl.BlockSpec((tb, qk), lambda i: (i, 0))],
        out_specs=pl.BlockSpec((tb, RET_V), lambda i: (i, 0)),
        out_shape=jax.ShapeDtypeStruct((t, RET_V), BF16),
        scratch_shapes=[pltpu.VMEM((nh, qk, dv), F32),
                        pltpu.VMEM((nh, tb, tb), F32),
                        pltpu.VMEM((nh, tb, qk), F32),
                        pltpu.VMEM((nh, tb, qk), F32)],
        compiler_params=_params("arbitrary"),
    )(rest, rest, rest, rest, cos, sin)


def _conformer_kernel(ca_ref, cb_ref, w_ref, b_ref, lnw_ref, lnb_ref, o_ref, upad_ref, sh_ref):
    tb = ca_ref.shape[0]
    rows = tb + HALO

    @pl.when(pl.program_id(0) == 0)
    def _():
        upad_ref[0:HALO, :] = jnp.zeros((HALO, CONV_DIM), F32)

    upad_ref[HALO:, :] = ca_ref[...] * jax.nn.sigmoid(cb_ref[...])
    full = upad_ref[...]
    for r in range(1, SUBLANES):
        sh_ref[r - 1] = pltpu.roll(full, rows - r, axis=0)
    base = HALO - (CONV_WIDTH - 1)
    acc = None
    for j in range(CONV_WIDTH):
        off = base + j
        r, a = off % SUBLANES, off - off % SUBLANES
        src = upad_ref[a:a + tb, :] if r == 0 else sh_ref[r - 1, a:a + tb, :]
        term = w_ref[j:j + 1, :] * src
        acc = term if acc is None else acc + term
    upad_ref[0:HALO, :] = upad_ref[tb:tb + HALO, :]
    u = acc + b_ref[...]
    mu = jnp.mean(u, axis=-1, keepdims=True)
    uc = u - mu
    u = uc * lax.rsqrt(jnp.mean(uc * uc, axis=-1, keepdims=True) + LN_EPS) * lnw_ref[...] + lnb_ref[...]
    o_ref[...] = _silu(u).astype(o_ref.dtype)


def _conformer(rest, w, b, ln_w, ln_b):
    t = rest.shape[0]
    tb = TB_CONV
    c = CONV_DIM
    vec = lambda: pl.BlockSpec((1, c), lambda i: (0, 0))
    return pl.pallas_call(
        _conformer_kernel,
        grid=(t // tb,),
        in_specs=[pl.BlockSpec((tb, c), lambda i: (i, R_GLU // c)),
                  pl.BlockSpec((tb, c), lambda i: (i, R_GLU // c + 1)),
                  pl.BlockSpec((CONV_WIDTH, c), lambda i: (0, 0)),
                  vec(), vec(), vec()],
        out_specs=pl.BlockSpec((tb, c), lambda i: (i, 0)),
        out_shape=jax.ShapeDtypeStruct((t, c), BF16),
        scratch_shapes=[pltpu.VMEM((tb + HALO, c), F32),
                        pltpu.VMEM((SUBLANES - 1, tb + HALO, c), F32)],
        compiler_params=_params("arbitrary"),
    )(rest, rest, w, b.reshape(1, c), ln_w.reshape(1, c), ln_b.reshape(1, c))


def _merge_kernel(oa_ref, ob_ref, oc_ref, wa_ref, wb_ref, wc_ref, ga_ref, gb_ref, gc_ref, o_ref,
                  wa16_ref, wb16_ref, wc16_ref):
    @pl.when(pl.program_id(1) == 0)
    def _():
        wa16_ref[...] = wa_ref[...].astype(BF16)
        wb16_ref[...] = wb_ref[...].astype(BF16)
        wc16_ref[...] = wc_ref[...].astype(BF16)
    m = ga_ref[...].astype(F32) * _dot(oa_ref[...], wa16_ref[...])
    m = m + gb_ref[...].astype(F32) * _dot(ob_ref[...], wb16_ref[...])
    m = m + gc_ref[...].astype(F32) * _dot(oc_ref[...], wc16_ref[...])
    o_ref[...] = m.astype(o_ref.dtype)


def _merge(o_a, o_b, o_c, w_a, w_b, w_c, layer, gates):
    t, k = o_a.shape
    n = w_a.shape[2]
    tm, tn = TM_MERGE, TN
    act = lambda: pl.BlockSpec((tm, k), lambda j, i: (i, 0))
    wsp = lambda: pl.BlockSpec((None, k, tn), lambda j, i: (layer, 0, j))
    gate = lambda b: pl.BlockSpec((tm, tn), lambda j, i: (i, b * n // tn + j))
    return pl.pallas_call(
        _merge_kernel,
        grid=(n // tn, t // tm),
        in_specs=[act(), act(), act(), wsp(), wsp(), wsp(), gate(0), gate(1), gate(2)],
        out_specs=pl.BlockSpec((tm, tn), lambda j, i: (i, j)),
        out_shape=jax.ShapeDtypeStruct((t, n), BF16),
        scratch_shapes=[pltpu.VMEM((k, tn), BF16)] * 3,
        compiler_params=_params("arbitrary", "arbitrary"),
    )(o_a, o_b, o_c, w_a, w_b, w_c, gates, gates, gates)


def kernel(x, c, positions, w_ada, b_ada, norm_mix_w, norm_mlp_w, w_in, conv_qkv_w, gdn_a_log, gdn_dt_bias,
           gdn_norm_w, conv_dw_w, conv_dw_b, conv_ln_w, conv_ln_b, w_branch_a, w_branch_b, w_branch_c,
           w_out, w_mlp_in, w_mlp_out, final_norm_w):
    bsz, t, d = x.shape
    assert bsz == 1
    xs = x.reshape(t, d)
    mod = _adaln_mod(c, w_ada, b_ada)
    cos, sin = _rope_tables(positions.reshape(t))
    w_in_t = jnp.swapaxes(w_in, 1, 2)
    for l in range(w_in.shape[0]):
        h = _norm_mod(xs, norm_mix_w[l], mod[l], 0, 1)
        qk = _proj(h, w_in_t, l, 0, 2 * GDN_DIM, "conv_l2", conv_qkv_w[l], 0)
        v = _proj(h, w_in_t, l, 2 * GDN_DIM, GDN_DIM, "conv", conv_qkv_w[l], 2 * GDN_DIM)
        z = _proj(h, w_in_t, l, 3 * GDN_DIM, GDN_DIM, "silu")
        rest = _proj(h, w_in_t, l, P_RQ + BA_COLS, R_GATE, "plain")
        gates = _proj(h, w_in_t, l, P_RQ + BA_COLS + R_GATE, R_WIDTH - R_GATE, "sigmoid", out_dtype=BF16)
        bg, bgt = _gdn_gates(h, w_in_t, l, gdn_a_log[l], gdn_dt_bias[l])
        o_a = _gdn(qk, v, z, bg, bgt, gdn_norm_w[l])
        o_b = _retention(rest, cos, sin)
        o_c = _conformer(rest, conv_dw_w[l], conv_dw_b[l], conv_ln_w[l], conv_ln_b[l])
        merged = _merge(o_a, o_b, o_c, w_branch_a, w_branch_b, w_branch_c, l, gates)
        xs = _matmul_residual(merged, w_out, l, xs, mod[l, 2:3])
        h = _norm_mod(xs, norm_mlp_w[l], mod[l], 3, 4)
        act = _matmul(h, w_mlp_in, l, tm=TM_WIDE, out_dtype=BF16, relu2=True)
        xs = _mlp_out(act, w_mlp_out, l, xs, mod[l, 5:6])
    return _final_norm(xs, final_norm_w).reshape(bsz, t, d)
```

```python
import functools
import math

import jax
import jax.numpy as jnp
from jax import lax
from jax.experimental import pallas as pl
from jax.experimental.pallas import tpu as pltpu

F32 = jnp.float32
BF16 = jnp.bfloat16

D_MODEL = 2048
GDN_HEADS = 8
GDN_HEAD_DIM = 128
GDN_DIM = GDN_HEADS * GDN_HEAD_DIM
SHORT_CONV = 4
RET_HEADS = 4
RET_QK_DIM = 128
RET_V_DIM = 256
RET_QK = RET_HEADS * RET_QK_DIM
RET_V = RET_HEADS * RET_V_DIM
RET_DECAY_BASE = 5.0
ROPE_BASE = 10000.0
CONV_DIM = D_MODEL // 2
CONV_WIDTH = 31
N_BRANCH = 3
NORM_EPS = 1e-6
LN_EPS = 1e-5
L2_EPS = 1e-6

BA_COLS = 2 * GDN_HEADS
P_RQ = 4 * GDN_DIM
R_RQ = 0
R_RK = R_RQ + RET_QK
R_RV = R_RK + RET_QK
R_RG = R_RV + RET_V
R_GLU = R_RG + RET_V
R_GATE = R_GLU + 2 * CONV_DIM
R_WIDTH = R_GATE + N_BRANCH * D_MODEL

LANES = 128
SUBLANES = 8
VMEM_LIMIT_BYTES = 60 * 1024 * 1024
TM = 1024
TM_WIDE = 2048
TM_PROJ = 2048
TM_MERGE = 512
TM_OUT = 512
TN = 1024
ROW_CHUNK = 256
GDN_CHUNK = 128
TB_GATE = 512
TB_RET = 256
TB_CONV = 256
TB_NORM = 1024
HALO = 32


def _params(*sem):
    return pltpu.CompilerParams(dimension_semantics=sem, vmem_limit_bytes=VMEM_LIMIT_BYTES)


def _dot(a, b):
    return jnp.dot(a, b, preferred_element_type=F32)


def _dot_nt(a, b):
    return lax.dot_general(a, b, (((1,), (1,)), ((), ())), preferred_element_type=F32)


def _dot_exact_lhs(m_bf16, g):
    g1 = g.astype(BF16)
    r1 = g - g1.astype(F32)
    g2 = r1.astype(BF16)
    g3 = (r1 - g2.astype(F32)).astype(BF16)
    return _dot(m_bf16, g1) + _dot(m_bf16, g2) + _dot(m_bf16, g3)


def _silu(x):
    return x * jax.nn.sigmoid(x)


def _mod_kernel(c_ref, w_ref, b_ref, o_ref):
    prod = _silu(c_ref[...]) * w_ref[0]
    d, tn = prod.shape
    part = jnp.sum(prod.reshape(d // SUBLANES, SUBLANES, tn), axis=0)
    o_ref[0] = jnp.sum(part, axis=0, keepdims=True) + b_ref[0]


def _adaln_mod(c, w_ada, b_ada):
    depth, d, n = w_ada.shape
    tn = 1024
    out = pl.pallas_call(
        _mod_kernel,
        grid=(depth, n // tn),
        in_specs=[pl.BlockSpec((d, 1), lambda l, j: (0, 0)),
                  pl.BlockSpec((1, d, tn), lambda l, j: (l, 0, j)),
                  pl.BlockSpec((1, 1, tn), lambda l, j: (l, 0, j))],
        out_specs=pl.BlockSpec((1, 1, tn), lambda l, j: (l, 0, j)),
        out_shape=jax.ShapeDtypeStruct((depth, 1, n), F32),
        compiler_params=_params("arbitrary", "arbitrary"),
    )(c.reshape(d, 1), w_ada, b_ada.reshape(depth, 1, n))
    return out.reshape(depth, 6, d)


def _norm_mod_kernel(x_ref, w_ref, mod_ref, o_ref, *, shift_row, scale_row):
    x = x_ref[...]
    y = x * lax.rsqrt(jnp.mean(x * x, axis=-1, keepdims=True) + NORM_EPS) * w_ref[...]
    y = y * (1.0 + mod_ref[scale_row:scale_row + 1, :]) + mod_ref[shift_row:shift_row + 1, :]
    o_ref[...] = y.astype(o_ref.dtype)


def _norm_mod(x, w, mod, shift_row, scale_row):
    t, d = x.shape
    return pl.pallas_call(
        functools.partial(_norm_mod_kernel, shift_row=shift_row, scale_row=scale_row),
        grid=(t // TB_NORM,),
        in_specs=[pl.BlockSpec((TB_NORM, d), lambda i: (i, 0)),
                  pl.BlockSpec((1, d), lambda i: (0, 0)),
                  pl.BlockSpec((6, d), lambda i: (0, 0))],
        out_specs=pl.BlockSpec((TB_NORM, d), lambda i: (i, 0)),
        out_shape=jax.ShapeDtypeStruct((t, d), BF16),
        compiler_params=_params("arbitrary"),
    )(x, w.reshape(1, d), mod)


def _final_norm_kernel(x_ref, w_ref, o_ref):
    x = x_ref[...]
    o_ref[...] = x * lax.rsqrt(jnp.mean(x * x, axis=-1, keepdims=True) + NORM_EPS) * w_ref[...]


def _final_norm(x, w):
    t, d = x.shape
    return pl.pallas_call(
        _final_norm_kernel,
        grid=(t // TB_NORM,),
        in_specs=[pl.BlockSpec((TB_NORM, d), lambda i: (i, 0)),
                  pl.BlockSpec((1, d), lambda i: (0, 0))],
        out_specs=pl.BlockSpec((TB_NORM, d), lambda i: (i, 0)),
        out_shape=jax.ShapeDtypeStruct((t, d), F32),
        compiler_params=_params("arbitrary"),
    )(x, w.reshape(1, d))


def _mm_kernel(a_ref, w_ref, o_ref, wb_ref):
    @pl.when(pl.program_id(1) == 0)
    def _():
        wb_ref[...] = w_ref[...].astype(BF16)
    o_ref[...] = _dot(a_ref[...], wb_ref[...]).astype(o_ref.dtype)


def _mm_relu2_kernel(a_ref, w_ref, o_ref, wb_ref):
    @pl.when(pl.program_id(1) == 0)
    def _():
        wb_ref[...] = w_ref[...].astype(BF16)
    y = jnp.maximum(_dot(a_ref[...], wb_ref[...]), 0.0)
    o_ref[...] = (y * y).astype(o_ref.dtype)


def _matmul(a, w, layer, *, tm=TM, out_dtype=F32, relu2=False):
    t, k = a.shape
    n = w.shape[2]
    tn = min(TN, n)
    tm = min(tm, t)
    return pl.pallas_call(
        _mm_relu2_kernel if relu2 else _mm_kernel,
        grid=(n // tn, t // tm),
        in_specs=[pl.BlockSpec((tm, k), lambda j, i: (i, 0)),
                  pl.BlockSpec((None, k, tn), lambda j, i: (layer, 0, j))],
        out_specs=pl.BlockSpec((tm, tn), lambda j, i: (i, j)),
        out_shape=jax.ShapeDtypeStruct((t, n), out_dtype),
        scratch_shapes=[pltpu.VMEM((k, tn), BF16)],
        compiler_params=_params("arbitrary", "arbitrary"),
    )(a, w)


def _out_norm_kernel(a_ref, w_ref, x_ref, g_ref, nw_ref, mod_ref, xo_ref, h_ref, wb_ref, *, shift_row, scale_row):
    @pl.when(pl.program_id(0) == 0)
    def _():
        wb_ref[...] = w_ref[...].astype(BF16)
    x = x_ref[...] + g_ref[...] * _dot(a_ref[...], wb_ref[...])
    xo_ref[...] = x
    y = x * lax.rsqrt(jnp.mean(x * x, axis=-1, keepdims=True) + NORM_EPS) * nw_ref[...]
    y = y * (1.0 + mod_ref[scale_row:scale_row + 1, :]) + mod_ref[shift_row:shift_row + 1, :]
    h_ref[...] = y.astype(h_ref.dtype)


def _out_proj_norm(a, w, layer, x, gate, norm_w, mod, shift_row, scale_row):
    t, k = a.shape
    n = w.shape[2]
    tm = TM_OUT
    row = lambda: pl.BlockSpec((tm, n), lambda i: (i, 0))
    return pl.pallas_call(
        functools.partial(_out_norm_kernel, shift_row=shift_row, scale_row=scale_row),
        grid=(t // tm,),
        in_specs=[pl.BlockSpec((tm, k), lambda i: (i, 0)),
                  pl.BlockSpec((None, k, n), lambda i: (layer, 0, 0), pipeline_mode=pl.Buffered(1)),
                  row(),
                  pl.BlockSpec((1, n), lambda i: (0, 0)),
                  pl.BlockSpec((1, n), lambda i: (0, 0)),
                  pl.BlockSpec((6, n), lambda i: (0, 0))],
        out_specs=[row(), row()],
        out_shape=[jax.ShapeDtypeStruct((t, n), F32), jax.ShapeDtypeStruct((t, n), BF16)],
        scratch_shapes=[pltpu.VMEM((k, n), BF16)],
        compiler_params=_params("arbitrary"),
    )(a, w, x, gate, norm_w.reshape(1, n), mod)


def _proj_kernel(*refs, mode, shifted):
    if mode in ("conv", "conv_l2"):
        a_ref, w_ref, cw_ref, o_ref, wb_ref, carry_ref = refs
    elif shifted:
        a_ref, w_ref, wn_ref, o_ref, wb_ref = refs
    else:
        a_ref, w_ref, o_ref, wb_ref = refs
    j = pl.program_id(0)
    tm, tn = o_ref.shape

    @pl.when(pl.program_id(1) == 0)
    def _():
        if shifted:
            keep = w_ref.shape[0] - BA_COLS
            wb_ref[0:keep, :] = w_ref[BA_COLS:, :].astype(BF16)
            wb_ref[keep:, :] = wn_ref[...].astype(BF16)
        else:
            wb_ref[...] = w_ref[...].astype(BF16)
        if mode in ("conv", "conv_l2"):
            carry_ref[...] = jnp.zeros_like(carry_ref)

    wb = wb_ref[...]
    hd = GDN_HEAD_DIM
    prev = carry_ref[...] if mode in ("conv", "conv_l2") else None
    for r in range(tm // ROW_CHUNK):
        rows = slice(r * ROW_CHUNK, (r + 1) * ROW_CHUNK)
        y = _dot_nt(a_ref[rows, :], wb)
        if mode == "plain":
            o_ref[rows, :] = y
        elif mode == "silu":
            o_ref[rows, :] = _silu(y)
        elif mode == "sigmoid":
            o_ref[rows, :] = jax.nn.sigmoid(y).astype(o_ref.dtype)
        else:
            full = jnp.concatenate([prev, y], axis=0)
            acc = cw_ref[SHORT_CONV - 1:SHORT_CONV, :] * y
            for tap in range(SHORT_CONV - 1):
                acc = acc + cw_ref[tap:tap + 1, :] * pltpu.roll(full, SHORT_CONV - 1 - tap, axis=0)[SUBLANES:]
            prev = y[ROW_CHUNK - SUBLANES:, :]
            s = _silu(acc)
            if mode == "conv":
                o_ref[rows, :] = s
            else:
                scale = jnp.where(j < GDN_DIM // tn, hd ** -0.5, 1.0)
                for g in range(tn // hd):
                    blk = s[:, g * hd:(g + 1) * hd]
                    inv = lax.rsqrt(jnp.sum(blk * blk, axis=-1, keepdims=True) + L2_EPS) * scale
                    o_ref[rows, g * hd:(g + 1) * hd] = blk * inv
    if mode in ("conv", "conv_l2"):
        carry_ref[...] = prev


def _proj(a, w_in_t, layer, row0, width, mode, conv_w=None, conv_col0=0, out_dtype=F32):
    t, k = a.shape
    tn = TN
    shifted = row0 >= P_RQ
    r0 = row0 - BA_COLS if shifted else row0
    assert r0 % tn == 0 and width % tn == 0
    tm = min(TM if mode in ("conv", "conv_l2") else TM_PROJ, t)
    in_specs = [pl.BlockSpec((tm, k), lambda j, i: (i, 0)),
                pl.BlockSpec((None, tn, k), lambda j, i: (layer, r0 // tn + j, 0))]
    args = [a, w_in_t]
    scratch = [pltpu.VMEM((tn, k), BF16)]
    if shifted:
        per = tn // BA_COLS
        in_specs.append(pl.BlockSpec((None, BA_COLS, k), lambda j, i: (layer, (r0 // tn + j + 1) * per, 0)))
        args.append(w_in_t)
    if mode in ("conv", "conv_l2"):
        in_specs.append(pl.BlockSpec((SHORT_CONV, tn), lambda j, i: (0, conv_col0 // tn + j)))
        args.append(conv_w)
        scratch.append(pltpu.VMEM((SUBLANES, tn), F32))
    return pl.pallas_call(
        functools.partial(_proj_kernel, mode=mode, shifted=shifted),
        grid=(width // tn, t // tm),
        in_specs=in_specs,
        out_specs=pl.BlockSpec((tm, tn), lambda j, i: (i, j)),
        out_shape=jax.ShapeDtypeStruct((t, width), out_dtype),
        scratch_shapes=scratch,
        compiler_params=_params("arbitrary", "arbitrary"),
    )(*args)


def _mlp_out_kernel(a_ref, w_ref, x_ref, g_ref, o_ref, acc_ref):
    kk = pl.program_id(2)

    @pl.when(kk == 0)
    def _():
        acc_ref[...] = jnp.zeros_like(acc_ref)
    acc_ref[...] += _dot(a_ref[...], w_ref[...].astype(BF16))

    @pl.when(kk == pl.num_programs(2) - 1)
    def _():
        o_ref[...] = x_ref[...] + g_ref[...] * acc_ref[...]


def _mlp_out(a, w, layer, x, gate):
    t, k = a.shape
    n = w.shape[2]
    tm, tn, tk = 1024, 1024, 2048
    return pl.pallas_call(
        _mlp_out_kernel,
        grid=(t // tm, n // tn, k // tk),
        in_specs=[pl.BlockSpec((tm, tk), lambda i, j, kk: (i, kk)),
                  pl.BlockSpec((None, tk, tn), lambda i, j, kk: (layer, kk, j)),
                  pl.BlockSpec((tm, tn), lambda i, j, kk: (i, j)),
                  pl.BlockSpec((1, tn), lambda i, j, kk: (0, j))],
        out_specs=pl.BlockSpec((tm, tn), lambda i, j, kk: (i, j)),
        out_shape=jax.ShapeDtypeStruct((t, n), F32),
        scratch_shapes=[pltpu.VMEM((tm, tn), F32)],
        compiler_params=_params("arbitrary", "arbitrary", "arbitrary"),
    )(a, w, x, gate)


def _gdn_gate_kernel(h_ref, w_ref, alog_ref, dtb_ref, bg_ref, bgt_ref):
    tb = h_ref.shape[0]
    ba = _dot_nt(h_ref[...], w_ref[...].astype(BF16))
    beta = jax.nn.sigmoid(ba)
    xs = ba + dtb_ref[...]
    softplus = jnp.maximum(xs, 0.0) + jnp.log1p(jnp.exp(-jnp.abs(xs)))
    g = -jnp.exp(alog_ref[...]) * softplus
    ck = GDN_CHUNK
    ii = lax.broadcasted_iota(jnp.int32, (ck, ck), 0)
    jj = lax.broadcasted_iota(jnp.int32, (ck, ck), 1)
    tri = jnp.where(ii >= jj, 1.0, 0.0).astype(BF16)
    ones = jnp.ones((ck, ck), BF16)
    chunks = [g[s * ck:(s + 1) * ck] for s in range(tb // ck)]
    gc = jnp.concatenate([_dot_exact_lhs(tri, gs) for gs in chunks], axis=0)
    gl = jnp.concatenate([_dot_exact_lhs(ones, gs) for gs in chunks], axis=0)
    lane = lax.broadcasted_iota(jnp.int32, (tb, LANES), 1)
    out = jnp.where(lane < GDN_HEADS, beta,
                    jnp.where(lane < 2 * GDN_HEADS, gc,
                              jnp.where(lane < 3 * GDN_HEADS, pltpu.roll(gl, GDN_HEADS, axis=1), 0.0)))
    bg_ref[...] = out
    bgt_ref[...] = out.T


def _gdn_gates(h, w_in_t, layer, a_log, dt_bias):
    t, d = h.shape
    tb = TB_GATE
    lead = jnp.zeros((GDN_HEADS,), F32)
    tail = jnp.zeros((LANES - 2 * GDN_HEADS,), F32)
    alog = jnp.concatenate([lead, a_log, tail]).reshape(1, LANES)
    dtb = jnp.concatenate([lead, dt_bias, tail]).reshape(1, LANES)
    return pl.pallas_call(
        _gdn_gate_kernel,
        grid=(t // tb,),
        in_specs=[pl.BlockSpec((tb, d), lambda i: (i, 0)),
                  pl.BlockSpec((None, LANES, d), lambda i: (layer, P_RQ // LANES, 0)),
                  pl.BlockSpec((1, LANES), lambda i: (0, 0)),
                  pl.BlockSpec((1, LANES), lambda i: (0, 0))],
        out_specs=[pl.BlockSpec((tb, LANES), lambda i: (i, 0)),
                   pl.BlockSpec((LANES, tb), lambda i: (0, i))],
        out_shape=[jax.ShapeDtypeStruct((t, LANES), F32),
                   jax.ShapeDtypeStruct((LANES, t), F32)],
        compiler_params=_params("arbitrary"),
    )(h, w_in_t, alog, dtb)


def _tri_inverse_all(lows):
    n = lows[0].shape[0]
    ii = lax.broadcasted_iota(jnp.int32, (n, n), 0)
    jj = lax.broadcasted_iota(jnp.int32, (n, n), 1)
    ts = None
    for level in range(int(math.log2(n))):
        rb = jnp.right_shift(ii, level)
        cb = jnp.right_shift(jj, level)
        sel = (jnp.bitwise_and(rb, 1) == 1) & (cb == rb - 1)
        if level == 0:
            eye = jnp.where(ii == jj, 1.0, 0.0)
            ts = [eye - jnp.where(sel, low, 0.0) for low in lows]
        else:
            t16 = [t.astype(BF16) for t in ts]
            ys = [_dot(jnp.where(sel, low, 0.0).astype(BF16), t).astype(BF16) for low, t in zip(lows, t16)]
            ts = [t - _dot(tb16, y) for t, tb16, y in zip(ts, t16, ys)]
    return ts


def _gdn_kernel(q_ref, k_ref, v_ref, z_ref, bg_ref, bgt_ref, nw_ref, o_ref, s_ref):
    tb = q_ref.shape[0]
    hd = GDN_HEAD_DIM

    @pl.when(pl.program_id(0) == 0)
    def _():
        s_ref[...] = jnp.zeros_like(s_ref)

    bg = bg_ref[...]
    bgt = bgt_ref[...]
    ii = lax.broadcasted_iota(jnp.int32, (tb, tb), 0)
    jj = lax.broadcasted_iota(jnp.int32, (tb, tb), 1)
    causal = ii >= jj
    strict = ii > jj
    nw = nw_ref[...]
    heads = range(GDN_HEADS)
    sl = [slice(h * hd, (h + 1) * hd) for h in heads]

    q = [q_ref[:, sl[h]] for h in heads]
    k = [k_ref[:, sl[h]] for h in heads]
    beta = [bg[:, h:h + 1] for h in heads]
    gc = [bg[:, GDN_HEADS + h:GDN_HEADS + h + 1] for h in heads]
    gl = [bg[:, 2 * GDN_HEADS + h:2 * GDN_HEADS + h + 1] for h in heads]
    eg = [jnp.exp(gc[h]) for h in heads]
    decay = [jnp.exp(jnp.where(causal, gc[h] - bgt[h:h + 1, :], -jnp.inf)) for h in heads]
    kb = [k[h] * beta[h] for h in heads]
    kq = [_dot_nt(jnp.concatenate([kb[h], q[h]], axis=0).astype(BF16), k[h].astype(BF16)) for h in heads]
    low = [jnp.where(strict, kq[h][:tb] * decay[h], 0.0) for h in heads]
    attn = [(kq[h][tb:] * decay[h]).astype(BF16) for h in heads]
    t_inv = _tri_inverse_all(low)
    rhs = [jnp.concatenate([v_ref[:, sl[h]] * beta[h], kb[h] * eg[h]], axis=1).astype(BF16) for h in heads]
    uw = [_dot(t_inv[h].astype(BF16), rhs[h]) for h in heads]
    state = [s_ref[h] for h in heads]
    ws = [_dot(jnp.concatenate([uw[h][:, hd:], q[h] * eg[h]], axis=0).astype(BF16), state[h].astype(BF16))
          for h in heads]
    v_new = [(uw[h][:, :hd] - ws[h][:tb]).astype(BF16) for h in heads]
    k_dec_t = [(k[h] * jnp.exp(gl[h] - gc[h])).T.astype(BF16) for h in heads]
    for h in heads:
        egl = jnp.broadcast_to(jnp.exp(gl[h]), (tb, hd))[0:1, :]
        s_ref[h] = state[h] * egl + _dot(k_dec_t[h], v_new[h])
    o = [ws[h][tb:] + _dot(attn[h], v_new[h]) for h in heads]
    o = [o[h] * lax.rsqrt(jnp.mean(o[h] * o[h], axis=-1, keepdims=True) + NORM_EPS) * nw for h in heads]
    o_ref[...] = jnp.concatenate([(o[h] * z_ref[:, sl[h]]).astype(o_ref.dtype) for h in heads], axis=1)


def _gdn(qk, v, z, bg, bgt, norm_w):
    t = qk.shape[0]
    tb = GDN_CHUNK
    gd, hd = GDN_DIM, GDN_HEAD_DIM
    col = lambda c: pl.BlockSpec((tb, gd), lambda i: (i, c))
    return pl.pallas_call(
        _gdn_kernel,
        grid=(t // tb,),
        in_specs=[col(0), col(1), col(0), col(0),
                  pl.BlockSpec((tb, LANES), lambda i: (i, 0)),
                  pl.BlockSpec((SUBLANES, tb), lambda i: (1, i)),
                  pl.BlockSpec((1, hd), lambda i: (0, 0))],
        out_specs=pl.BlockSpec((tb, gd), lambda i: (i, 0)),
        out_shape=jax.ShapeDtypeStruct((t, gd), BF16),
        scratch_shapes=[pltpu.VMEM((GDN_HEADS, hd, hd), F32)],
        compiler_params=_params("arbitrary"),
    )(qk, qk, v, z, bg, bgt, norm_w.reshape(1, hd))


def _rope_kernel(pos_ref, cos_ref, sin_ref):
    half = RET_QK_DIM // 2
    lane = lax.broadcasted_iota(jnp.int32, (1, RET_QK_DIM), 1)
    idx = jnp.where(lane < half, lane, lane - half).astype(F32)
    inv_freq = jnp.exp(idx * (-math.log(ROPE_BASE) / half))
    ang = pos_ref[...].astype(F32) * inv_freq
    cos_ref[...] = jnp.cos(ang)
    sin_ref[...] = jnp.where(lane < half, -1.0, 1.0) * jnp.sin(ang)


def _rope_tables(positions):
    t = positions.shape[0]
    tb = 1024
    return pl.pallas_call(
        _rope_kernel,
        grid=(t // tb,),
        in_specs=[pl.BlockSpec((tb, 1), lambda i: (i, 0))],
        out_specs=[pl.BlockSpec((tb, RET_QK_DIM), lambda i: (i, 0))] * 2,
        out_shape=[jax.ShapeDtypeStruct((t, RET_QK_DIM), F32)] * 2,
        compiler_params=_params("arbitrary"),
    )(positions.reshape(t, 1))


def _ret_kernel(q_ref, k_ref, v_ref, g_ref, cos_ref, sin_ref, o_ref, r_ref, dmask_ref, qs_ref, ks_ref):
    tb = q_ref.shape[0]
    qk, dv = RET_QK_DIM, RET_V_DIM
    heads = range(RET_HEADS)
    log_gamma = [math.log(1.0 - 2.0 ** (-RET_DECAY_BASE - h)) for h in heads]

    @pl.when(pl.program_id(0) == 0)
    def _():
        r_ref[...] = jnp.zeros_like(r_ref)
        ii = lax.broadcasted_iota(jnp.int32, (tb, tb), 0)
        jj = lax.broadcasted_iota(jnp.int32, (tb, tb), 1)
        rel = (ii - jj).astype(F32)
        idx = lax.broadcasted_iota(jnp.int32, (tb, qk), 0).astype(F32)
        for h in heads:
            dmask_ref[h] = jnp.where(rel >= 0.0, jnp.exp(jnp.maximum(rel, 0.0) * log_gamma[h]), 0.0)
            qs_ref[h] = jnp.exp((idx + 1.0) * log_gamma[h])
            ks_ref[h] = jnp.exp((tb - 1.0 - idx) * log_gamma[h])

    cos, sin = cos_ref[...], sin_ref[...]
    rope = lambda x: x * cos + pltpu.roll(x, qk // 2, axis=1) * sin
    q = [rope(q_ref[:, h * qk:(h + 1) * qk]) for h in heads]
    k = [rope(k_ref[:, h * qk:(h + 1) * qk]) * (qk ** -0.5) for h in heads]
    v16 = [v_ref[:, h * dv:(h + 1) * dv].astype(BF16) for h in heads]
    scores = [(_dot_nt(q[h].astype(BF16), k[h].astype(BF16)) * dmask_ref[h]).astype(BF16) for h in heads]
    state = [r_ref[h] for h in heads]
    o = [_dot(scores[h], v16[h]) + _dot((q[h] * qs_ref[h]).astype(BF16), state[h].astype(BF16)) for h in heads]
    for h in heads:
        r_ref[h] = state[h] * math.exp(tb * log_gamma[h]) + _dot((k[h] * ks_ref[h]).T.astype(BF16), v16[h])
    outs = []
    for h in heads:
        oc = o[h] - jnp.mean(o[h], axis=-1, keepdims=True)
        on = oc * lax.rsqrt(jnp.mean(oc * oc, axis=-1, keepdims=True) + LN_EPS)
        outs.append((on * _silu(g_ref[:, h * dv:(h + 1) * dv])).astype(o_ref.dtype))
    o_ref[...] = jnp.concatenate(outs, axis=1)


def _retention(rest, cos, sin):
    t = rest.shape[0]
    tb = TB_RET
    qk, dv, nh = RET_QK_DIM, RET_V_DIM, RET_HEADS
    return pl.pallas_call(
        _ret_kernel,
        grid=(t // tb,),
        in_specs=[pl.BlockSpec((tb, RET_QK), lambda i: (i, R_RQ // RET_QK)),
                  pl.BlockSpec((tb, RET_QK), lambda i: (i, R_RK // RET_QK)),
                  pl.BlockSpec((tb, RET_V), lambda i: (i, R_RV // RET_V)),
                  pl.BlockSpec((tb, RET_V), lambda i: (i, R_RG // RET_V)),
                  pl.BlockSpec((tb, qk), lambda i: (i, 0)),
                  pl.BlockSpec((tb, qk), lambda i: (i, 0))],
        out_specs=pl.BlockSpec((tb, RET_V), lambda i: (i, 0)),
        out_shape=jax.ShapeDtypeStruct((t, RET_V), BF16),
        scratch_shapes=[pltpu.VMEM((nh, qk, dv), F32),
                        pltpu.VMEM((nh, tb, tb), F32),
                        pltpu.VMEM((nh, tb, qk), F32),
                        pltpu.VMEM((nh, tb, qk), F32)],
        compiler_params=_params("arbitrary"),
    )(rest, rest, rest, rest, cos, sin)


def _conformer_kernel(ca_ref, cb_ref, w_ref, b_ref, lnw_ref, lnb_ref, o_ref, upad_ref, sh_ref):
    tb = ca_ref.shape[0]
    rows = tb + HALO

    @pl.when(pl.program_id(0) == 0)
    def _():
        upad_ref[0:HALO, :] = jnp.zeros((HALO, CONV_DIM), F32)

    upad_ref[HALO:, :] = ca_ref[...] * jax.nn.sigmoid(cb_ref[...])
    full = upad_ref[...]
    for r in range(1, SUBLANES):
        sh_ref[r - 1] = pltpu.roll(full, rows - r, axis=0)
    base = HALO - (CONV_WIDTH - 1)
    acc = None
    for j in range(CONV_WIDTH):
        off = base + j
        r, a = off % SUBLANES, off - off % SUBLANES
        src = upad_ref[a:a + tb, :] if r == 0 else sh_ref[r - 1, a:a + tb, :]
        term = w_ref[j:j + 1, :] * src
        acc = term if acc is None else acc + term
    upad_ref[0:HALO, :] = upad_ref[tb:tb + HALO, :]
    u = acc + b_ref[...]
    mu = jnp.mean(u, axis=-1, keepdims=True)
    uc = u - mu
    u = uc * lax.rsqrt(jnp.mean(uc * uc, axis=-1, keepdims=True) + LN_EPS) * lnw_ref[...] + lnb_ref[...]
    o_ref[...] = _silu(u).astype(o_ref.dtype)


def _conformer(rest, w, b, ln_w, ln_b):
    t = rest.shape[0]
    tb = TB_CONV
    c = CONV_DIM
    vec = lambda: pl.BlockSpec((1, c), lambda i: (0, 0))
    return pl.pallas_call(
        _conformer_kernel,
        grid=(t // tb,),
        in_specs=[pl.BlockSpec((tb, c), lambda i: (i, R_GLU // c)),
                  pl.BlockSpec((tb, c), lambda i: (i, R_GLU // c + 1)),
                  pl.BlockSpec((CONV_WIDTH, c), lambda i: (0, 0)),
                  vec(), vec(), vec()],
        out_specs=pl.BlockSpec((tb, c), lambda i: (i, 0)),
        out_shape=jax.ShapeDtypeStruct((t, c), BF16),
        scratch_shapes=[pltpu.VMEM((tb + HALO, c), F32),
                        pltpu.VMEM((SUBLANES - 1, tb + HALO, c), F32)],
        compiler_params=_params("arbitrary"),
    )(rest, rest, w, b.reshape(1, c), ln_w.reshape(1, c), ln_b.reshape(1, c))


def _merge_kernel(oa_ref, ob_ref, oc_ref, wa_ref, wb_ref, wc_ref, ga_ref, gb_ref, gc_ref, o_ref,
                  wa16_ref, wb16_ref, wc16_ref):
    @pl.when(pl.program_id(1) == 0)
    def _():
        wa16_ref[...] = wa_ref[...].astype(BF16)
        wb16_ref[...] = wb_ref[...].astype(BF16)
        wc16_ref[...] = wc_ref[...].astype(BF16)
    m = ga_ref[...].astype(F32) * _dot(oa_ref[...], wa16_ref[...])
    m = m + gb_ref[...].astype(F32) * _dot(ob_ref[...], wb16_ref[...])
    m = m + gc_ref[...].astype(F32) * _dot(oc_ref[...], wc16_ref[...])
    o_ref[...] = m.astype(o_ref.dtype)


def _merge(o_a, o_b, o_c, w_a, w_b, w_c, layer, gates):
    t, k = o_a.shape
    n = w_a.shape[2]
    tm, tn = TM_MERGE, TN
    act = lambda: pl.BlockSpec((tm, k), lambda j, i: (i, 0))
    wsp = lambda: pl.BlockSpec((None, k, tn), lambda j, i: (layer, 0, j))
    gate = lambda b: pl.BlockSpec((tm, tn), lambda j, i: (i, b * n // tn + j))
    return pl.pallas_call(
        _merge_kernel,
        grid=(n // tn, t // tm),
        in_specs=[act(), act(), act(), wsp(), wsp(), wsp(), gate(0), gate(1), gate(2)],
        out_specs=pl.BlockSpec((tm, tn), lambda j, i: (i, j)),
        out_shape=jax.ShapeDtypeStruct((t, n), BF16),
        scratch_shapes=[pltpu.VMEM((k, tn), BF16)] * 3,
        compiler_params=_params("arbitrary", "arbitrary"),
    )(o_a, o_b, o_c, w_a, w_b, w_c, gates, gates, gates)


def kernel(x, c, positions, w_ada, b_ada, norm_mix_w, norm_mlp_w, w_in, conv_qkv_w, gdn_a_log, gdn_dt_bias,
           gdn_norm_w, conv_dw_w, conv_dw_b, conv_ln_w, conv_ln_b, w_branch_a, w_branch_b, w_branch_c,
           w_out, w_mlp_in, w_mlp_out, final_norm_w):
    bsz, t, d = x.shape
    assert bsz == 1
    xs = x.reshape(t, d)
    mod = _adaln_mod(c, w_ada, b_ada)
    cos, sin = _rope_tables(positions.reshape(t))
    w_in_t = jnp.swapaxes(w_in, 1, 2)
    for l in range(w_in.shape[0]):
        h = _norm_mod(xs, norm_mix_w[l], mod[l], 0, 1)
        qk = _proj(h, w_in_t, l, 0, 2 * GDN_DIM, "conv_l2", conv_qkv_w[l], 0)
        v = _proj(h, w_in_t, l, 2 * GDN_DIM, GDN_DIM, "conv", conv_qkv_w[l], 2 * GDN_DIM)
        z = _proj(h, w_in_t, l, 3 * GDN_DIM, GDN_DIM, "silu")
        rest = _proj(h, w_in_t, l, P_RQ + BA_COLS, R_GATE, "plain")
        gates = _proj(h, w_in_t, l, P_RQ + BA_COLS + R_GATE, R_WIDTH - R_GATE, "sigmoid", out_dtype=BF16)
        bg, bgt = _gdn_gates(h, w_in_t, l, gdn_a_log[l], gdn_dt_bias[l])
        o_a = _gdn(qk, v, z, bg, bgt, gdn_norm_w[l])
        o_b = _retention(rest, cos, sin)
        o_c = _conformer(rest, conv_dw_w[l], conv_dw_b[l], conv_ln_w[l], conv_ln_b[l])
        merged = _merge(o_a, o_b, o_c, w_branch_a, w_branch_b, w_branch_c, l, gates)
        xs, h = _out_proj_norm(merged, w_out, l, xs, mod[l, 2:3], norm_mlp_w[l], mod[l], 3, 4)
        act = _matmul(h, w_mlp_in, l, tm=TM_WIDE, out_dtype=BF16, relu2=True)
        xs = _mlp_out(act, w_mlp_out, l, xs, mod[l, 5:6])
    return _final_norm(xs, final_norm_w).reshape(bsz, t, d)
```

```python
import functools
import math

import jax
import jax.numpy as jnp
from jax import lax
from jax.experimental import pallas as pl
from jax.experimental.pallas import tpu as pltpu

F32 = jnp.float32
BF16 = jnp.bfloat16

D_MODEL = 2048
GDN_HEADS = 8
GDN_HEAD_DIM = 128
GDN_DIM = GDN_HEADS * GDN_HEAD_DIM
SHORT_CONV = 4
RET_HEADS = 4
RET_QK_DIM = 128
RET_V_DIM = 256
RET_QK = RET_HEADS * RET_QK_DIM
RET_V = RET_HEADS * RET_V_DIM
RET_DECAY_BASE = 5.0
ROPE_BASE = 10000.0
CONV_DIM = D_MODEL // 2
CONV_WIDTH = 31
N_BRANCH = 3
NORM_EPS = 1e-6
LN_EPS = 1e-5
L2_EPS = 1e-6

BA_COLS = 2 * GDN_HEADS
P_RQ = 4 * GDN_DIM
R_RQ = 0
R_RK = R_RQ + RET_QK
R_RV = R_RK + RET_QK
R_RG = R_RV + RET_V
R_GLU = R_RG + RET_V
R_GATE = R_GLU + 2 * CONV_DIM
R_WIDTH = R_GATE + N_BRANCH * D_MODEL

LANES = 128
SUBLANES = 8
VMEM_LIMIT_BYTES = 60 * 1024 * 1024
TM = 1024
TM_WIDE = 2048
TM_PROJ = 2048
TM_MERGE = 512
TM_OUT = 512
TN = 1024
ROW_CHUNK = 256
GDN_CHUNK = 128
TB_GDN = 512
TB_GATE = 512
TB_RET = 256
TB_CONV = 256
TB_NORM = 1024
HALO = 32


def _params(*sem):
    return pltpu.CompilerParams(dimension_semantics=sem, vmem_limit_bytes=VMEM_LIMIT_BYTES)


def _dot(a, b):
    return jnp.dot(a, b, preferred_element_type=F32)


def _dot_nt(a, b):
    return lax.dot_general(a, b, (((1,), (1,)), ((), ())), preferred_element_type=F32)


def _dot_exact_lhs(m_bf16, g):
    g1 = g.astype(BF16)
    r1 = g - g1.astype(F32)
    g2 = r1.astype(BF16)
    g3 = (r1 - g2.astype(F32)).astype(BF16)
    return _dot(m_bf16, g1) + _dot(m_bf16, g2) + _dot(m_bf16, g3)


def _silu(x):
    return x * jax.nn.sigmoid(x)


def _mod_kernel(c_ref, w_ref, b_ref, o_ref):
    prod = _silu(c_ref[...]) * w_ref[0]
    d, tn = prod.shape
    part = jnp.sum(prod.reshape(d // SUBLANES, SUBLANES, tn), axis=0)
    o_ref[0] = jnp.sum(part, axis=0, keepdims=True) + b_ref[0]


def _adaln_mod(c, w_ada, b_ada):
    depth, d, n = w_ada.shape
    tn = 1024
    out = pl.pallas_call(
        _mod_kernel,
        grid=(depth, n // tn),
        in_specs=[pl.BlockSpec((d, 1), lambda l, j: (0, 0)),
                  pl.BlockSpec((1, d, tn), lambda l, j: (l, 0, j)),
                  pl.BlockSpec((1, 1, tn), lambda l, j: (l, 0, j))],
        out_specs=pl.BlockSpec((1, 1, tn), lambda l, j: (l, 0, j)),
        out_shape=jax.ShapeDtypeStruct((depth, 1, n), F32),
        compiler_params=_params("arbitrary", "arbitrary"),
    )(c.reshape(d, 1), w_ada, b_ada.reshape(depth, 1, n))
    return out.reshape(depth, 6, d)


def _norm_mod_kernel(x_ref, w_ref, mod_ref, o_ref, *, shift_row, scale_row):
    x = x_ref[...]
    y = x * lax.rsqrt(jnp.mean(x * x, axis=-1, keepdims=True) + NORM_EPS) * w_ref[...]
    y = y * (1.0 + mod_ref[scale_row:scale_row + 1, :]) + mod_ref[shift_row:shift_row + 1, :]
    o_ref[...] = y.astype(o_ref.dtype)


def _norm_mod(x, w, mod, shift_row, scale_row):
    t, d = x.shape
    return pl.pallas_call(
        functools.partial(_norm_mod_kernel, shift_row=shift_row, scale_row=scale_row),
        grid=(t // TB_NORM,),
        in_specs=[pl.BlockSpec((TB_NORM, d), lambda i: (i, 0)),
                  pl.BlockSpec((1, d), lambda i: (0, 0)),
                  pl.BlockSpec((6, d), lambda i: (0, 0))],
        out_specs=pl.BlockSpec((TB_NORM, d), lambda i: (i, 0)),
        out_shape=jax.ShapeDtypeStruct((t, d), BF16),
        compiler_params=_params("arbitrary"),
    )(x, w.reshape(1, d), mod)


def _final_norm_kernel(x_ref, w_ref, o_ref):
    x = x_ref[...]
    o_ref[...] = x * lax.rsqrt(jnp.mean(x * x, axis=-1, keepdims=True) + NORM_EPS) * w_ref[...]


def _final_norm(x, w):
    t, d = x.shape
    return pl.pallas_call(
        _final_norm_kernel,
        grid=(t // TB_NORM,),
        in_specs=[pl.BlockSpec((TB_NORM, d), lambda i: (i, 0)),
                  pl.BlockSpec((1, d), lambda i: (0, 0))],
        out_specs=pl.BlockSpec((TB_NORM, d), lambda i: (i, 0)),
        out_shape=jax.ShapeDtypeStruct((t, d), F32),
        compiler_params=_params("arbitrary"),
    )(x, w.reshape(1, d))


def _mm_kernel(a_ref, w_ref, o_ref, wb_ref):
    @pl.when(pl.program_id(1) == 0)
    def _():
        wb_ref[...] = w_ref[...].astype(BF16)
    o_ref[...] = _dot(a_ref[...], wb_ref[...]).astype(o_ref.dtype)


def _mm_relu2_kernel(a_ref, w_ref, o_ref, wb_ref):
    @pl.when(pl.program_id(1) == 0)
    def _():
        wb_ref[...] = w_ref[...].astype(BF16)
    y = jnp.maximum(_dot(a_ref[...], wb_ref[...]), 0.0)
    o_ref[...] = (y * y).astype(o_ref.dtype)


def _matmul(a, w, layer, *, tm=TM, out_dtype=F32, relu2=False):
    t, k = a.shape
    n = w.shape[2]
    tn = min(TN, n)
    tm = min(tm, t)
    return pl.pallas_call(
        _mm_relu2_kernel if relu2 else _mm_kernel,
        grid=(n // tn, t // tm),
        in_specs=[pl.BlockSpec((tm, k), lambda j, i: (i, 0)),
                  pl.BlockSpec((None, k, tn), lambda j, i: (layer, 0, j))],
        out_specs=pl.BlockSpec((tm, tn), lambda j, i: (i, j)),
        out_shape=jax.ShapeDtypeStruct((t, n), out_dtype),
        scratch_shapes=[pltpu.VMEM((k, tn), BF16)],
        compiler_params=_params("arbitrary", "arbitrary"),
    )(a, w)


def _out_norm_kernel(a_ref, w_ref, x_ref, g_ref, nw_ref, mod_ref, xo_ref, h_ref, wb_ref, *, shift_row, scale_row):
    @pl.when(pl.program_id(0) == 0)
    def _():
        wb_ref[...] = w_ref[...].astype(BF16)
    x = x_ref[...] + g_ref[...] * _dot(a_ref[...], wb_ref[...])
    xo_ref[...] = x
    y = x * lax.rsqrt(jnp.mean(x * x, axis=-1, keepdims=True) + NORM_EPS) * nw_ref[...]
    y = y * (1.0 + mod_ref[scale_row:scale_row + 1, :]) + mod_ref[shift_row:shift_row + 1, :]
    h_ref[...] = y.astype(h_ref.dtype)


def _out_proj_norm(a, w, layer, x, gate, norm_w, mod, shift_row, scale_row):
    t, k = a.shape
    n = w.shape[2]
    tm = TM_OUT
    row = lambda: pl.BlockSpec((tm, n), lambda i: (i, 0))
    return pl.pallas_call(
        functools.partial(_out_norm_kernel, shift_row=shift_row, scale_row=scale_row),
        grid=(t // tm,),
        in_specs=[pl.BlockSpec((tm, k), lambda i: (i, 0)),
                  pl.BlockSpec((None, k, n), lambda i: (layer, 0, 0), pipeline_mode=pl.Buffered(1)),
                  row(),
                  pl.BlockSpec((1, n), lambda i: (0, 0)),
                  pl.BlockSpec((1, n), lambda i: (0, 0)),
                  pl.BlockSpec((6, n), lambda i: (0, 0))],
        out_specs=[row(), row()],
        out_shape=[jax.ShapeDtypeStruct((t, n), F32), jax.ShapeDtypeStruct((t, n), BF16)],
        scratch_shapes=[pltpu.VMEM((k, n), BF16)],
        compiler_params=_params("arbitrary"),
    )(a, w, x, gate, norm_w.reshape(1, n), mod)


def _proj_kernel(*refs, mode, shifted):
    if mode in ("conv", "conv_l2"):
        a_ref, w_ref, cw_ref, o_ref, wb_ref, carry_ref = refs
    elif shifted:
        a_ref, w_ref, wn_ref, o_ref, wb_ref = refs
    else:
        a_ref, w_ref, o_ref, wb_ref = refs
    j = pl.program_id(0)
    tm, tn = o_ref.shape

    @pl.when(pl.program_id(1) == 0)
    def _():
        if shifted:
            keep = w_ref.shape[0] - BA_COLS
            wb_ref[0:keep, :] = w_ref[BA_COLS:, :].astype(BF16)
            wb_ref[keep:, :] = wn_ref[...].astype(BF16)
        else:
            wb_ref[...] = w_ref[...].astype(BF16)
        if mode in ("conv", "conv_l2"):
            carry_ref[...] = jnp.zeros_like(carry_ref)

    wb = wb_ref[...]
    hd = GDN_HEAD_DIM
    prev = carry_ref[...] if mode in ("conv", "conv_l2") else None
    for r in range(tm // ROW_CHUNK):
        rows = slice(r * ROW_CHUNK, (r + 1) * ROW_CHUNK)
        y = _dot_nt(a_ref[rows, :], wb)
        if mode == "plain":
            o_ref[rows, :] = y
        elif mode == "silu":
            o_ref[rows, :] = _silu(y)
        elif mode == "sigmoid":
            o_ref[rows, :] = jax.nn.sigmoid(y).astype(o_ref.dtype)
        else:
            full = jnp.concatenate([prev, y], axis=0)
            acc = cw_ref[SHORT_CONV - 1:SHORT_CONV, :] * y
            for tap in range(SHORT_CONV - 1):
                acc = acc + cw_ref[tap:tap + 1, :] * pltpu.roll(full, SHORT_CONV - 1 - tap, axis=0)[SUBLANES:]
            prev = y[ROW_CHUNK - SUBLANES:, :]
            s = _silu(acc)
            if mode == "conv":
                o_ref[rows, :] = s
            else:
                scale = jnp.where(j < GDN_DIM // tn, hd ** -0.5, 1.0)
                for g in range(tn // hd):
                    blk = s[:, g * hd:(g + 1) * hd]
                    inv = lax.rsqrt(jnp.sum(blk * blk, axis=-1, keepdims=True) + L2_EPS) * scale
                    o_ref[rows, g * hd:(g + 1) * hd] = blk * inv
    if mode in ("conv", "conv_l2"):
        carry_ref[...] = prev


def _proj(a, w_in_t, layer, row0, width, mode, conv_w=None, conv_col0=0, out_dtype=F32):
    t, k = a.shape
    tn = TN
    shifted = row0 >= P_RQ
    r0 = row0 - BA_COLS if shifted else row0
    assert r0 % tn == 0 and width % tn == 0
    tm = min(TM if mode in ("conv", "conv_l2") else TM_PROJ, t)
    in_specs = [pl.BlockSpec((tm, k), lambda j, i: (i, 0)),
                pl.BlockSpec((None, tn, k), lambda j, i: (layer, r0 // tn + j, 0))]
    args = [a, w_in_t]
    scratch = [pltpu.VMEM((tn, k), BF16)]
    if shifted:
        per = tn // BA_COLS
        in_specs.append(pl.BlockSpec((None, BA_COLS, k), lambda j, i: (layer, (r0 // tn + j + 1) * per, 0)))
        args.append(w_in_t)
    if mode in ("conv", "conv_l2"):
        in_specs.append(pl.BlockSpec((SHORT_CONV, tn), lambda j, i: (0, conv_col0 // tn + j)))
        args.append(conv_w)
        scratch.append(pltpu.VMEM((SUBLANES, tn), F32))
    return pl.pallas_call(
        functools.partial(_proj_kernel, mode=mode, shifted=shifted),
        grid=(width // tn, t // tm),
        in_specs=in_specs,
        out_specs=pl.BlockSpec((tm, tn), lambda j, i: (i, j)),
        out_shape=jax.ShapeDtypeStruct((t, width), out_dtype),
        scratch_shapes=scratch,
        compiler_params=_params("arbitrary", "arbitrary"),
    )(*args)


def _mlp_out_kernel(a_ref, w_ref, x_ref, g_ref, o_ref, acc_ref):
    kk = pl.program_id(2)

    @pl.when(kk == 0)
    def _():
        acc_ref[...] = jnp.zeros_like(acc_ref)
    acc_ref[...] += _dot(a_ref[...], w_ref[...].astype(BF16))

    @pl.when(kk == pl.num_programs(2) - 1)
    def _():
        o_ref[...] = x_ref[...] + g_ref[...] * acc_ref[...]


def _mlp_out(a, w, layer, x, gate):
    t, k = a.shape
    n = w.shape[2]
    tm, tn, tk = 1024, 1024, 2048
    return pl.pallas_call(
        _mlp_out_kernel,
        grid=(t // tm, n // tn, k // tk),
        in_specs=[pl.BlockSpec((tm, tk), lambda i, j, kk: (i, kk)),
                  pl.BlockSpec((None, tk, tn), lambda i, j, kk: (layer, kk, j)),
                  pl.BlockSpec((tm, tn), lambda i, j, kk: (i, j)),
                  pl.BlockSpec((1, tn), lambda i, j, kk: (0, j))],
        out_specs=pl.BlockSpec((tm, tn), lambda i, j, kk: (i, j)),
        out_shape=jax.ShapeDtypeStruct((t, n), F32),
        scratch_shapes=[pltpu.VMEM((tm, tn), F32)],
        compiler_params=_params("arbitrary", "arbitrary", "arbitrary"),
    )(a, w, x, gate)


def _gdn_gate_kernel(h_ref, w_ref, alog_ref, dtb_ref, bg_ref, bgt_ref):
    tb = h_ref.shape[0]
    ba = _dot_nt(h_ref[...], w_ref[...].astype(BF16))
    beta = jax.nn.sigmoid(ba)
    xs = ba + dtb_ref[...]
    softplus = jnp.maximum(xs, 0.0) + jnp.log1p(jnp.exp(-jnp.abs(xs)))
    g = -jnp.exp(alog_ref[...]) * softplus
    ck = GDN_CHUNK
    ii = lax.broadcasted_iota(jnp.int32, (ck, ck), 0)
    jj = lax.broadcasted_iota(jnp.int32, (ck, ck), 1)
    tri = jnp.where(ii >= jj, 1.0, 0.0).astype(BF16)
    ones = jnp.ones((ck, ck), BF16)
    chunks = [g[s * ck:(s + 1) * ck] for s in range(tb // ck)]
    gc = jnp.concatenate([_dot_exact_lhs(tri, gs) for gs in chunks], axis=0)
    gl = jnp.concatenate([_dot_exact_lhs(ones, gs) for gs in chunks], axis=0)
    lane = lax.broadcasted_iota(jnp.int32, (tb, LANES), 1)
    out = jnp.where(lane < GDN_HEADS, beta,
                    jnp.where(lane < 2 * GDN_HEADS, gc,
                              jnp.where(lane < 3 * GDN_HEADS, pltpu.roll(gl, GDN_HEADS, axis=1), 0.0)))
    bg_ref[...] = out
    bgt_ref[...] = out.T


def _gdn_gates(h, w_in_t, layer, a_log, dt_bias):
    t, d = h.shape
    tb = TB_GATE
    lead = jnp.zeros((GDN_HEADS,), F32)
    tail = jnp.zeros((LANES - 2 * GDN_HEADS,), F32)
    alog = jnp.concatenate([lead, a_log, tail]).reshape(1, LANES)
    dtb = jnp.concatenate([lead, dt_bias, tail]).reshape(1, LANES)
    return pl.pallas_call(
        _gdn_gate_kernel,
        grid=(t // tb,),
        in_specs=[pl.BlockSpec((tb, d), lambda i: (i, 0)),
                  pl.BlockSpec((None, LANES, d), lambda i: (layer, P_RQ // LANES, 0)),
                  pl.BlockSpec((1, LANES), lambda i: (0, 0)),
                  pl.BlockSpec((1, LANES), lambda i: (0, 0))],
        out_specs=[pl.BlockSpec((tb, LANES), lambda i: (i, 0)),
                   pl.BlockSpec((LANES, tb), lambda i: (0, i))],
        out_shape=[jax.ShapeDtypeStruct((t, LANES), F32),
                   jax.ShapeDtypeStruct((LANES, t), F32)],
        compiler_params=_params("arbitrary"),
    )(h, w_in_t, alog, dtb)


def _tri_inverse_all(lows):
    n = lows[0].shape[0]
    ii = lax.broadcasted_iota(jnp.int32, (n, n), 0)
    jj = lax.broadcasted_iota(jnp.int32, (n, n), 1)
    ts = None
    for level in range(int(math.log2(n))):
        rb = jnp.right_shift(ii, level)
        cb = jnp.right_shift(jj, level)
        sel = (jnp.bitwise_and(rb, 1) == 1) & (cb == rb - 1)
        if level == 0:
            eye = jnp.where(ii == jj, 1.0, 0.0)
            ts = [eye - jnp.where(sel, low, 0.0) for low in lows]
        else:
            t16 = [t.astype(BF16) for t in ts]
            ys = [_dot(jnp.where(sel, low, 0.0).astype(BF16), t).astype(BF16) for low, t in zip(lows, t16)]
            ts = [t - _dot(tb16, y) for t, tb16, y in zip(ts, t16, ys)]
    return ts


def _gdn_kernel(q_ref, k_ref, v_ref, z_ref, bg_ref, bgt_ref, nw_ref, o_ref, s_ref):
    ck = GDN_CHUNK
    hd = GDN_HEAD_DIM
    n_chunks = q_ref.shape[0] // ck

    @pl.when(pl.program_id(0) == 0)
    def _():
        s_ref[...] = jnp.zeros_like(s_ref)

    ii = lax.broadcasted_iota(jnp.int32, (ck, ck), 0)
    jj = lax.broadcasted_iota(jnp.int32, (ck, ck), 1)
    causal = ii >= jj
    strict = ii > jj
    nw = nw_ref[...]
    heads = range(GDN_HEADS)
    items = [(c, h) for c in range(n_chunks) for h in heads]
    rows = lambda c: slice(c * ck, (c + 1) * ck)
    cols = lambda h: slice(h * hd, (h + 1) * hd)

    q = [q_ref[rows(c), cols(h)] for c, h in items]
    k = [k_ref[rows(c), cols(h)] for c, h in items]
    beta = [bg_ref[rows(c), h:h + 1] for c, h in items]
    gc = [bg_ref[rows(c), GDN_HEADS + h:GDN_HEADS + h + 1] for c, h in items]
    gl = [bg_ref[rows(c), 2 * GDN_HEADS + h:2 * GDN_HEADS + h + 1] for c, h in items]
    gc_row = [bgt_ref[h:h + 1, rows(c)] for c, h in items]
    n_items = range(len(items))
    eg = [jnp.exp(gc[n]) for n in n_items]
    decay = [jnp.exp(jnp.where(causal, gc[n] - gc_row[n], -jnp.inf)) for n in n_items]
    kb = [k[n] * beta[n] for n in n_items]
    kq = [_dot_nt(jnp.concatenate([kb[n], q[n]], axis=0).astype(BF16), k[n].astype(BF16)) for n in n_items]
    low = [jnp.where(strict, kq[n][:ck] * decay[n], 0.0) for n in n_items]
    attn = [(kq[n][ck:] * decay[n]).astype(BF16) for n in n_items]
    t_inv = _tri_inverse_all(low)
    rhs = [jnp.concatenate([v_ref[rows(c), cols(h)] * beta[n], kb[n] * eg[n]], axis=1).astype(BF16)
           for n, (c, h) in enumerate(items)]
    uw = [_dot(t_inv[n].astype(BF16), rhs[n]) for n in n_items]
    wq = [jnp.concatenate([uw[n][:, hd:], q[n] * eg[n]], axis=0).astype(BF16) for n in n_items]
    k_dec_t = [(k[n] * jnp.exp(gl[n] - gc[n])).T.astype(BF16) for n in n_items]
    egl = [jnp.broadcast_to(jnp.exp(gl[n]), (ck, hd))[0:1, :] for n in n_items]

    state = [s_ref[h] for h in heads]
    for c in range(n_chunks):
        base = c * GDN_HEADS
        ws = [_dot(wq[base + h], state[h].astype(BF16)) for h in heads]
        v_new = [(uw[base + h][:, :hd] - ws[h][:ck]).astype(BF16) for h in heads]
        state = [state[h] * egl[base + h] + _dot(k_dec_t[base + h], v_new[h]) for h in heads]
        o = [ws[h][ck:] + _dot(attn[base + h], v_new[h]) for h in heads]
        o = [o[h] * lax.rsqrt(jnp.mean(o[h] * o[h], axis=-1, keepdims=True) + NORM_EPS) * nw for h in heads]
        o_ref[rows(c), :] = jnp.concatenate(
            [(o[h] * z_ref[rows(c), cols(h)]).astype(o_ref.dtype) for h in heads], axis=1)
    for h in heads:
        s_ref[h] = state[h]


def _gdn(qk, v, z, bg, bgt, norm_w):
    t = qk.shape[0]
    tb = TB_GDN
    gd, hd = GDN_DIM, GDN_HEAD_DIM
    col = lambda c: pl.BlockSpec((tb, gd), lambda i: (i, c))
    return pl.pallas_call(
        _gdn_kernel,
        grid=(t // tb,),
        in_specs=[col(0), col(1), col(0), col(0),
                  pl.BlockSpec((tb, LANES), lambda i: (i, 0)),
                  pl.BlockSpec((SUBLANES, tb), lambda i: (1, i)),
                  pl.BlockSpec((1, hd), lambda i: (0, 0))],
        out_specs=pl.BlockSpec((tb, gd), lambda i: (i, 0)),
        out_shape=jax.ShapeDtypeStruct((t, gd), BF16),
        scratch_shapes=[pltpu.VMEM((GDN_HEADS, hd, hd), F32)],
        compiler_params=_params("arbitrary"),
    )(qk, qk, v, z, bg, bgt, norm_w.reshape(1, hd))


def _rope_kernel(pos_ref, cos_ref, sin_ref):
    half = RET_QK_DIM // 2
    lane = lax.broadcasted_iota(jnp.int32, (1, RET_QK_DIM), 1)
    idx = jnp.where(lane < half, lane, lane - half).astype(F32)
    inv_freq = jnp.exp(idx * (-math.log(ROPE_BASE) / half))
    ang = pos_ref[...].astype(F32) * inv_freq
    cos_ref[...] = jnp.cos(ang)
    sin_ref[...] = jnp.where(lane < half, -1.0, 1.0) * jnp.sin(ang)


def _rope_tables(positions):
    t = positions.shape[0]
    tb = 1024
    return pl.pallas_call(
        _rope_kernel,
        grid=(t // tb,),
        in_specs=[pl.BlockSpec((tb, 1), lambda i: (i, 0))],
        out_specs=[pl.BlockSpec((tb, RET_QK_DIM), lambda i: (i, 0))] * 2,
        out_shape=[jax.ShapeDtypeStruct((t, RET_QK_DIM), F32)] * 2,
        compiler_params=_params("arbitrary"),
    )(positions.reshape(t, 1))


def _ret_kernel(q_ref, k_ref, v_ref, g_ref, cos_ref, sin_ref, o_ref, r_ref, dmask_ref, qs_ref, ks_ref):
    tb = q_ref.shape[0]
    qk, dv = RET_QK_DIM, RET_V_DIM
    heads = range(RET_HEADS)
    log_gamma = [math.log(1.0 - 2.0 ** (-RET_DECAY_BASE - h)) for h in heads]

    @pl.when(pl.program_id(0) == 0)
    def _():
        r_ref[...] = jnp.zeros_like(r_ref)
        ii = lax.broadcasted_iota(jnp.int32, (tb, tb), 0)
        jj = lax.broadcasted_iota(jnp.int32, (tb, tb), 1)
        rel = (ii - jj).astype(F32)
        idx = lax.broadcasted_iota(jnp.int32, (tb, qk), 0).astype(F32)
        for h in heads:
            dmask_ref[h] = jnp.where(rel >= 0.0, jnp.exp(jnp.maximum(rel, 0.0) * log_gamma[h]), 0.0)
            qs_ref[h] = jnp.exp((idx + 1.0) * log_gamma[h])
            ks_ref[h] = jnp.exp((tb - 1.0 - idx) * log_gamma[h])

    cos, sin = cos_ref[...], sin_ref[...]
    rope = lambda x: x * cos + pltpu.roll(x, qk // 2, axis=1) * sin
    q = [rope(q_ref[:, h * qk:(h + 1) * qk]) for h in heads]
    k = [rope(k_ref[:, h * qk:(h + 1) * qk]) * (qk ** -0.5) for h in heads]
    v16 = [v_ref[:, h * dv:(h + 1) * dv].astype(BF16) for h in heads]
    scores = [(_dot_nt(q[h].astype(BF16), k[h].astype(BF16)) * dmask_ref[h]).astype(BF16) for h in heads]
    state = [r_ref[h] for h in heads]
    o = [_dot(scores[h], v16[h]) + _dot((q[h] * qs_ref[h]).astype(BF16), state[h].astype(BF16)) for h in heads]
    for h in heads:
        r_ref[h] = state[h] * math.exp(tb * log_gamma[h]) + _dot((k[h] * ks_ref[h]).T.astype(BF16), v16[h])
    outs = []
    for h in heads:
        oc = o[h] - jnp.mean(o[h], axis=-1, keepdims=True)
        on = oc * lax.rsqrt(jnp.mean(oc * oc, axis=-1, keepdims=True) + LN_EPS)
        outs.append((on * _silu(g_ref[:, h * dv:(h + 1) * dv])).astype(o_ref.dtype))
    o_ref[...] = jnp.concatenate(outs, axis=1)


def _retention(rest, cos, sin):
    t = rest.shape[0]
    tb = TB_RET
    qk, dv, nh = RET_QK_DIM, RET_V_DIM, RET_HEADS
    return pl.pallas_call(
        _ret_kernel,
        grid=(t // tb,),
        in_specs=[pl.BlockSpec((tb, RET_QK), lambda i: (i, R_RQ // RET_QK)),
                  pl.BlockSpec((tb, RET_QK), lambda i: (i, R_RK // RET_QK)),
                  pl.BlockSpec((tb, RET_V), lambda i: (i, R_RV // RET_V)),
                  pl.BlockSpec((tb, RET_V), lambda i: (i, R_RG // RET_V)),
                  pl.BlockSpec((tb, qk), lambda i: (i, 0)),
                  pl.BlockSpec((tb, qk), lambda i: (i, 0))],
        out_specs=pl.BlockSpec((tb, RET_V), lambda i: (i, 0)),
        out_shape=jax.ShapeDtypeStruct((t, RET_V), BF16),
        scratch_shapes=[pltpu.VMEM((nh, qk, dv), F32),
                        pltpu.VMEM((nh, tb, tb), F32),
                        pltpu.VMEM((nh, tb, qk), F32),
                        pltpu.VMEM((nh, tb, qk), F32)],
        compiler_params=_params("arbitrary"),
    )(rest, rest, rest, rest, cos, sin)


def _conformer_kernel(ca_ref, cb_ref, w_ref, b_ref, lnw_ref, lnb_ref, o_ref, upad_ref, sh_ref):
    tb = ca_ref.shape[0]
    rows = tb + HALO

    @pl.when(pl.program_id(0) == 0)
    def _():
        upad_ref[0:HALO, :] = jnp.zeros((HALO, CONV_DIM), F32)

    upad_ref[HALO:, :] = ca_ref[...] * jax.nn.sigmoid(cb_ref[...])
    full = upad_ref[...]
    for r in range(1, SUBLANES):
        sh_ref[r - 1] = pltpu.roll(full, rows - r, axis=0)
    base = HALO - (CONV_WIDTH - 1)
    acc = None
    for j in range(CONV_WIDTH):
        off = base + j
        r, a = off % SUBLANES, off - off % SUBLANES
        src = upad_ref[a:a + tb, :] if r == 0 else sh_ref[r - 1, a:a + tb, :]
        term = w_ref[j:j + 1, :] * src
        acc = term if acc is None else acc + term
    upad_ref[0:HALO, :] = upad_ref[tb:tb + HALO, :]
    u = acc + b_ref[...]
    mu = jnp.mean(u, axis=-1, keepdims=True)
    uc = u - mu
    u = uc * lax.rsqrt(jnp.mean(uc * uc, axis=-1, keepdims=True) + LN_EPS) * lnw_ref[...] + lnb_ref[...]
    o_ref[...] = _silu(u).astype(o_ref.dtype)


def _conformer(rest, w, b, ln_w, ln_b):
    t = rest.shape[0]
    tb = TB_CONV
    c = CONV_DIM
    vec = lambda: pl.BlockSpec((1, c), lambda i: (0, 0))
    return pl.pallas_call(
        _conformer_kernel,
        grid=(t // tb,),
        in_specs=[pl.BlockSpec((tb, c), lambda i: (i, R_GLU // c)),
                  pl.BlockSpec((tb, c), lambda i: (i, R_GLU // c + 1)),
                  pl.BlockSpec((CONV_WIDTH, c), lambda i: (0, 0)),
                  vec(), vec(), vec()],
        out_specs=pl.BlockSpec((tb, c), lambda i: (i, 0)),
        out_shape=jax.ShapeDtypeStruct((t, c), BF16),
        scratch_shapes=[pltpu.VMEM((tb + HALO, c), F32),
                        pltpu.VMEM((SUBLANES - 1, tb + HALO, c), F32)],
        compiler_params=_params("arbitrary"),
    )(rest, rest, w, b.reshape(1, c), ln_w.reshape(1, c), ln_b.reshape(1, c))


def _merge_kernel(oa_ref, ob_ref, oc_ref, wa_ref, wb_ref, wc_ref, ga_ref, gb_ref, gc_ref, o_ref,
                  wa16_ref, wb16_ref, wc16_ref):
    @pl.when(pl.program_id(1) == 0)
    def _():
        wa16_ref[...] = wa_ref[...].astype(BF16)
        wb16_ref[...] = wb_ref[...].astype(BF16)
        wc16_ref[...] = wc_ref[...].astype(BF16)
    m = ga_ref[...].astype(F32) * _dot(oa_ref[...], wa16_ref[...])
    m = m + gb_ref[...].astype(F32) * _dot(ob_ref[...], wb16_ref[...])
    m = m + gc_ref[...].astype(F32) * _dot(oc_ref[...], wc16_ref[...])
    o_ref[...] = m.astype(o_ref.dtype)


def _merge(o_a, o_b, o_c, w_a, w_b, w_c, layer, gates):
    t, k = o_a.shape
    n = w_a.shape[2]
    tm, tn = TM_MERGE, TN
    act = lambda: pl.BlockSpec((tm, k), lambda j, i: (i, 0))
    wsp = lambda: pl.BlockSpec((None, k, tn), lambda j, i: (layer, 0, j))
    gate = lambda b: pl.BlockSpec((tm, tn), lambda j, i: (i, b * n // tn + j))
    return pl.pallas_call(
        _merge_kernel,
        grid=(n // tn, t // tm),
        in_specs=[act(), act(), act(), wsp(), wsp(), wsp(), gate(0), gate(1), gate(2)],
        out_specs=pl.BlockSpec((tm, tn), lambda j, i: (i, j)),
        out_shape=jax.ShapeDtypeStruct((t, n), BF16),
        scratch_shapes=[pltpu.VMEM((k, tn), BF16)] * 3,
        compiler_params=_params("arbitrary", "arbitrary"),
    )(o_a, o_b, o_c, w_a, w_b, w_c, gates, gates, gates)


def kernel(x, c, positions, w_ada, b_ada, norm_mix_w, norm_mlp_w, w_in, conv_qkv_w, gdn_a_log, gdn_dt_bias,
           gdn_norm_w, conv_dw_w, conv_dw_b, conv_ln_w, conv_ln_b, w_branch_a, w_branch_b, w_branch_c,
           w_out, w_mlp_in, w_mlp_out, final_norm_w):
    bsz, t, d = x.shape
    assert bsz == 1
    xs = x.reshape(t, d)
    mod = _adaln_mod(c, w_ada, b_ada)
    cos, sin = _rope_tables(positions.reshape(t))
    w_in_t = jnp.swapaxes(w_in, 1, 2)
    for l in range(w_in.shape[0]):
        h = _norm_mod(xs, norm_mix_w[l], mod[l], 0, 1)
        qk = _proj(h, w_in_t, l, 0, 2 * GDN_DIM, "conv_l2", conv_qkv_w[l], 0)
        v = _proj(h, w_in_t, l, 2 * GDN_DIM, GDN_DIM, "conv", conv_qkv_w[l], 2 * GDN_DIM)
        z = _proj(h, w_in_t, l, 3 * GDN_DIM, GDN_DIM, "silu")
        rest = _proj(h, w_in_t, l, P_RQ + BA_COLS, R_GATE, "plain")
        gates = _proj(h, w_in_t, l, P_RQ + BA_COLS + R_GATE, R_WIDTH - R_GATE, "sigmoid", out_dtype=BF16)
        bg, bgt = _gdn_gates(h, w_in_t, l, gdn_a_log[l], gdn_dt_bias[l])
        o_a = _gdn(qk, v, z, bg, bgt, gdn_norm_w[l])
        o_b = _retention(rest, cos, sin)
        o_c = _conformer(rest, conv_dw_w[l], conv_dw_b[l], conv_ln_w[l], conv_ln_b[l])
        merged = _merge(o_a, o_b, o_c, w_branch_a, w_branch_b, w_branch_c, l, gates)
        xs, h = _out_proj_norm(merged, w_out, l, xs, mod[l, 2:3], norm_mlp_w[l], mod[l], 3, 4)
        act = _matmul(h, w_mlp_in, l, tm=TM_WIDE, out_dtype=BF16, relu2=True)
        xs = _mlp_out(act, w_mlp_out, l, xs, mod[l, 5:6])
    return _final_norm(xs, final_norm_w).reshape(bsz, t, d)
```

```python
import functools
import math

import jax
import jax.numpy as jnp
from jax import lax
from jax.experimental import pallas as pl
from jax.experimental.pallas import tpu as pltpu

F32 = jnp.float32
BF16 = jnp.bfloat16

D_MODEL = 2048
GDN_HEADS = 8
GDN_HEAD_DIM = 128
GDN_DIM = GDN_HEADS * GDN_HEAD_DIM
SHORT_CONV = 4
RET_HEADS = 4
RET_QK_DIM = 128
RET_V_DIM = 256
RET_QK = RET_HEADS * RET_QK_DIM
RET_V = RET_HEADS * RET_V_DIM
RET_DECAY_BASE = 5.0
ROPE_BASE = 10000.0
CONV_DIM = D_MODEL // 2
CONV_WIDTH = 31
N_BRANCH = 3
NORM_EPS = 1e-6
LN_EPS = 1e-5
L2_EPS = 1e-6

BA_COLS = 2 * GDN_HEADS
P_RQ = 4 * GDN_DIM
R_RQ = 0
R_RK = R_RQ + RET_QK
R_RV = R_RK + RET_QK
R_RG = R_RV + RET_V
R_GLU = R_RG + RET_V
R_GATE = R_GLU + 2 * CONV_DIM
R_WIDTH = R_GATE + N_BRANCH * D_MODEL

LANES = 128
SUBLANES = 8
VMEM_LIMIT_BYTES = 60 * 1024 * 1024
TM = 1024
TM_WIDE = 2048
TM_PROJ = 2048
TM_MERGE = 1024
TN_MERGE = 512
TM_OUT = 512
TN = 1024
ROW_CHUNK = 256
GDN_CHUNK = 128
TB_GDN = 512
TB_GATE = 1024
TB_RET = 256
TB_CONV = 512
TB_NORM = 1024
HALO = 32


def _params(*sem):
    return pltpu.CompilerParams(dimension_semantics=sem, vmem_limit_bytes=VMEM_LIMIT_BYTES)


def _dot(a, b):
    return jnp.dot(a, b, preferred_element_type=F32)


def _dot_nt(a, b):
    return lax.dot_general(a, b, (((1,), (1,)), ((), ())), preferred_element_type=F32)


def _dot_exact_lhs(m_bf16, g):
    g1 = g.astype(BF16)
    r1 = g - g1.astype(F32)
    g2 = r1.astype(BF16)
    g3 = (r1 - g2.astype(F32)).astype(BF16)
    return _dot(m_bf16, g1) + _dot(m_bf16, g2) + _dot(m_bf16, g3)


def _silu(x):
    return x * jax.nn.sigmoid(x)


def _mod_kernel(c_ref, w_ref, b_ref, o_ref):
    prod = _silu(c_ref[...]) * w_ref[0]
    d, tn = prod.shape
    part = jnp.sum(prod.reshape(d // SUBLANES, SUBLANES, tn), axis=0)
    o_ref[0] = jnp.sum(part, axis=0, keepdims=True) + b_ref[0]


def _adaln_mod(c, w_ada, b_ada):
    depth, d, n = w_ada.shape
    tn = 1024
    out = pl.pallas_call(
        _mod_kernel,
        grid=(depth, n // tn),
        in_specs=[pl.BlockSpec((d, 1), lambda l, j: (0, 0)),
                  pl.BlockSpec((1, d, tn), lambda l, j: (l, 0, j)),
                  pl.BlockSpec((1, 1, tn), lambda l, j: (l, 0, j))],
        out_specs=pl.BlockSpec((1, 1, tn), lambda l, j: (l, 0, j)),
        out_shape=jax.ShapeDtypeStruct((depth, 1, n), F32),
        compiler_params=_params("arbitrary", "arbitrary"),
    )(c.reshape(d, 1), w_ada, b_ada.reshape(depth, 1, n))
    return out.reshape(depth, 6, d)


def _norm_mod_kernel(x_ref, w_ref, mod_ref, o_ref, *, shift_row, scale_row):
    x = x_ref[...]
    y = x * lax.rsqrt(jnp.mean(x * x, axis=-1, keepdims=True) + NORM_EPS) * w_ref[...]
    y = y * (1.0 + mod_ref[scale_row:scale_row + 1, :]) + mod_ref[shift_row:shift_row + 1, :]
    o_ref[...] = y.astype(o_ref.dtype)


def _norm_mod(x, w, mod, shift_row, scale_row):
    t, d = x.shape
    return pl.pallas_call(
        functools.partial(_norm_mod_kernel, shift_row=shift_row, scale_row=scale_row),
        grid=(t // TB_NORM,),
        in_specs=[pl.BlockSpec((TB_NORM, d), lambda i: (i, 0)),
                  pl.BlockSpec((1, d), lambda i: (0, 0)),
                  pl.BlockSpec((6, d), lambda i: (0, 0))],
        out_specs=pl.BlockSpec((TB_NORM, d), lambda i: (i, 0)),
        out_shape=jax.ShapeDtypeStruct((t, d), BF16),
        compiler_params=_params("arbitrary"),
    )(x, w.reshape(1, d), mod)


def _final_norm_kernel(x_ref, w_ref, o_ref):
    x = x_ref[...]
    o_ref[...] = x * lax.rsqrt(jnp.mean(x * x, axis=-1, keepdims=True) + NORM_EPS) * w_ref[...]


def _final_norm(x, w):
    t, d = x.shape
    return pl.pallas_call(
        _final_norm_kernel,
        grid=(t // TB_NORM,),
        in_specs=[pl.BlockSpec((TB_NORM, d), lambda i: (i, 0)),
                  pl.BlockSpec((1, d), lambda i: (0, 0))],
        out_specs=pl.BlockSpec((TB_NORM, d), lambda i: (i, 0)),
        out_shape=jax.ShapeDtypeStruct((t, d), F32),
        compiler_params=_params("arbitrary"),
    )(x, w.reshape(1, d))


def _mm_kernel(a_ref, w_ref, o_ref, wb_ref):
    @pl.when(pl.program_id(1) == 0)
    def _():
        wb_ref[...] = w_ref[...].astype(BF16)
    o_ref[...] = _dot(a_ref[...], wb_ref[...]).astype(o_ref.dtype)


def _mm_relu2_kernel(a_ref, w_ref, o_ref, wb_ref):
    @pl.when(pl.program_id(1) == 0)
    def _():
        wb_ref[...] = w_ref[...].astype(BF16)
    y = jnp.maximum(_dot(a_ref[...], wb_ref[...]), 0.0)
    o_ref[...] = (y * y).astype(o_ref.dtype)


def _matmul(a, w, layer, *, tm=TM, out_dtype=F32, relu2=False):
    t, k = a.shape
    n = w.shape[2]
    tn = min(TN, n)
    tm = min(tm, t)
    return pl.pallas_call(
        _mm_relu2_kernel if relu2 else _mm_kernel,
        grid=(n // tn, t // tm),
        in_specs=[pl.BlockSpec((tm, k), lambda j, i: (i, 0)),
                  pl.BlockSpec((None, k, tn), lambda j, i: (layer, 0, j))],
        out_specs=pl.BlockSpec((tm, tn), lambda j, i: (i, j)),
        out_shape=jax.ShapeDtypeStruct((t, n), out_dtype),
        scratch_shapes=[pltpu.VMEM((k, tn), BF16)],
        compiler_params=_params("arbitrary", "arbitrary"),
    )(a, w)


def _out_norm_kernel(a_ref, w_ref, x_ref, g_ref, nw_ref, mod_ref, xo_ref, h_ref, wb_ref, *, shift_row, scale_row):
    @pl.when(pl.program_id(0) == 0)
    def _():
        wb_ref[...] = w_ref[...].astype(BF16)
    x = x_ref[...] + g_ref[...] * _dot(a_ref[...], wb_ref[...])
    xo_ref[...] = x
    y = x * lax.rsqrt(jnp.mean(x * x, axis=-1, keepdims=True) + NORM_EPS) * nw_ref[...]
    y = y * (1.0 + mod_ref[scale_row:scale_row + 1, :]) + mod_ref[shift_row:shift_row + 1, :]
    h_ref[...] = y.astype(h_ref.dtype)


def _out_proj_norm(a, w, layer, x, gate, norm_w, mod, shift_row, scale_row):
    t, k = a.shape
    n = w.shape[2]
    tm = TM_OUT
    row = lambda: pl.BlockSpec((tm, n), lambda i: (i, 0))
    return pl.pallas_call(
        functools.partial(_out_norm_kernel, shift_row=shift_row, scale_row=scale_row),
        grid=(t // tm,),
        in_specs=[pl.BlockSpec((tm, k), lambda i: (i, 0)),
                  pl.BlockSpec((None, k, n), lambda i: (layer, 0, 0), pipeline_mode=pl.Buffered(1)),
                  row(),
                  pl.BlockSpec((1, n), lambda i: (0, 0)),
                  pl.BlockSpec((1, n), lambda i: (0, 0)),
                  pl.BlockSpec((6, n), lambda i: (0, 0))],
        out_specs=[row(), row()],
        out_shape=[jax.ShapeDtypeStruct((t, n), F32), jax.ShapeDtypeStruct((t, n), BF16)],
        scratch_shapes=[pltpu.VMEM((k, n), BF16)],
        compiler_params=_params("arbitrary"),
    )(a, w, x, gate, norm_w.reshape(1, n), mod)


def _proj_kernel(*refs, mode, shifted):
    if mode in ("conv", "conv_l2"):
        a_ref, w_ref, cw_ref, o_ref, wb_ref, carry_ref = refs
    elif shifted:
        a_ref, w_ref, wn_ref, o_ref, wb_ref = refs
    else:
        a_ref, w_ref, o_ref, wb_ref = refs
    j = pl.program_id(0)
    tm, tn = o_ref.shape

    @pl.when(pl.program_id(1) == 0)
    def _():
        if shifted:
            keep = w_ref.shape[0] - BA_COLS
            wb_ref[0:keep, :] = w_ref[BA_COLS:, :].astype(BF16)
            wb_ref[keep:, :] = wn_ref[...].astype(BF16)
        else:
            wb_ref[...] = w_ref[...].astype(BF16)
        if mode in ("conv", "conv_l2"):
            carry_ref[...] = jnp.zeros_like(carry_ref)

    wb = wb_ref[...]
    hd = GDN_HEAD_DIM
    prev = carry_ref[...] if mode in ("conv", "conv_l2") else None
    for r in range(tm // ROW_CHUNK):
        rows = slice(r * ROW_CHUNK, (r + 1) * ROW_CHUNK)
        y = _dot_nt(a_ref[rows, :], wb)
        if mode == "plain":
            o_ref[rows, :] = y
        elif mode == "silu":
            o_ref[rows, :] = _silu(y)
        elif mode == "sigmoid":
            o_ref[rows, :] = jax.nn.sigmoid(y).astype(o_ref.dtype)
        else:
            full = jnp.concatenate([prev, y], axis=0)
            acc = cw_ref[SHORT_CONV - 1:SHORT_CONV, :] * y
            for tap in range(SHORT_CONV - 1):
                acc = acc + cw_ref[tap:tap + 1, :] * pltpu.roll(full, SHORT_CONV - 1 - tap, axis=0)[SUBLANES:]
            prev = y[ROW_CHUNK - SUBLANES:, :]
            s = _silu(acc)
            if mode == "conv":
                o_ref[rows, :] = s
            else:
                scale = jnp.where(j < GDN_DIM // tn, hd ** -0.5, 1.0)
                for g in range(tn // hd):
                    blk = s[:, g * hd:(g + 1) * hd]
                    inv = lax.rsqrt(jnp.sum(blk * blk, axis=-1, keepdims=True) + L2_EPS) * scale
                    o_ref[rows, g * hd:(g + 1) * hd] = blk * inv
    if mode in ("conv", "conv_l2"):
        carry_ref[...] = prev


def _proj(a, w_in_t, layer, row0, width, mode, conv_w=None, conv_col0=0, out_dtype=F32):
    t, k = a.shape
    tn = TN
    shifted = row0 >= P_RQ
    r0 = row0 - BA_COLS if shifted else row0
    assert r0 % tn == 0 and width % tn == 0
    tm = min(TM_PROJ, t)
    in_specs = [pl.BlockSpec((tm, k), lambda j, i: (i, 0)),
                pl.BlockSpec((None, tn, k), lambda j, i: (layer, r0 // tn + j, 0))]
    args = [a, w_in_t]
    scratch = [pltpu.VMEM((tn, k), BF16)]
    if shifted:
        per = tn // BA_COLS
        in_specs.append(pl.BlockSpec((None, BA_COLS, k), lambda j, i: (layer, (r0 // tn + j + 1) * per, 0)))
        args.append(w_in_t)
    if mode in ("conv", "conv_l2"):
        in_specs.append(pl.BlockSpec((SHORT_CONV, tn), lambda j, i: (0, conv_col0 // tn + j)))
        args.append(conv_w)
        scratch.append(pltpu.VMEM((SUBLANES, tn), F32))
    return pl.pallas_call(
        functools.partial(_proj_kernel, mode=mode, shifted=shifted),
        grid=(width // tn, t // tm),
        in_specs=in_specs,
        out_specs=pl.BlockSpec((tm, tn), lambda j, i: (i, j)),
        out_shape=jax.ShapeDtypeStruct((t, width), out_dtype),
        scratch_shapes=scratch,
        compiler_params=_params("arbitrary", "arbitrary"),
    )(*args)


def _mlp_out_kernel(a_ref, w_ref, x_ref, g_ref, o_ref, acc_ref):
    kk = pl.program_id(2)

    @pl.when(kk == 0)
    def _():
        acc_ref[...] = jnp.zeros_like(acc_ref)
    acc_ref[...] += _dot(a_ref[...], w_ref[...].astype(BF16))

    @pl.when(kk == pl.num_programs(2) - 1)
    def _():
        o_ref[...] = x_ref[...] + g_ref[...] * acc_ref[...]


def _mlp_out(a, w, layer, x, gate):
    t, k = a.shape
    n = w.shape[2]
    tm, tn, tk = 1024, 1024, 2048
    return pl.pallas_call(
        _mlp_out_kernel,
        grid=(t // tm, n // tn, k // tk),
        in_specs=[pl.BlockSpec((tm, tk), lambda i, j, kk: (i, kk)),
                  pl.BlockSpec((None, tk, tn), lambda i, j, kk: (layer, kk, j)),
                  pl.BlockSpec((tm, tn), lambda i, j, kk: (i, j)),
                  pl.BlockSpec((1, tn), lambda i, j, kk: (0, j))],
        out_specs=pl.BlockSpec((tm, tn), lambda i, j, kk: (i, j)),
        out_shape=jax.ShapeDtypeStruct((t, n), F32),
        scratch_shapes=[pltpu.VMEM((tm, tn), F32)],
        compiler_params=_params("arbitrary", "arbitrary", "arbitrary"),
    )(a, w, x, gate)


def _gdn_gate_kernel(h_ref, w_ref, alog_ref, dtb_ref, bg_ref, bgt_ref):
    tb = h_ref.shape[0]
    ba = _dot_nt(h_ref[...], w_ref[...].astype(BF16))
    beta = jax.nn.sigmoid(ba)
    xs = ba + dtb_ref[...]
    softplus = jnp.maximum(xs, 0.0) + jnp.log1p(jnp.exp(-jnp.abs(xs)))
    g = -jnp.exp(alog_ref[...]) * softplus
    ck = GDN_CHUNK
    ii = lax.broadcasted_iota(jnp.int32, (ck, ck), 0)
    jj = lax.broadcasted_iota(jnp.int32, (ck, ck), 1)
    tri = jnp.where(ii >= jj, 1.0, 0.0).astype(BF16)
    ones = jnp.ones((ck, ck), BF16)
    chunks = [g[s * ck:(s + 1) * ck] for s in range(tb // ck)]
    gc = jnp.concatenate([_dot_exact_lhs(tri, gs) for gs in chunks], axis=0)
    gl = jnp.concatenate([_dot_exact_lhs(ones, gs) for gs in chunks], axis=0)
    lane = lax.broadcasted_iota(jnp.int32, (tb, LANES), 1)
    out = jnp.where(lane < GDN_HEADS, beta,
                    jnp.where(lane < 2 * GDN_HEADS, gc,
                              jnp.where(lane < 3 * GDN_HEADS, pltpu.roll(gl, GDN_HEADS, axis=1), 0.0)))
    bg_ref[...] = out
    bgt_ref[...] = out.T


def _gdn_gates(h, w_in_t, layer, a_log, dt_bias):
    t, d = h.shape
    tb = TB_GATE
    lead = jnp.zeros((GDN_HEADS,), F32)
    tail = jnp.zeros((LANES - 2 * GDN_HEADS,), F32)
    alog = jnp.concatenate([lead, a_log, tail]).reshape(1, LANES)
    dtb = jnp.concatenate([lead, dt_bias, tail]).reshape(1, LANES)
    return pl.pallas_call(
        _gdn_gate_kernel,
        grid=(t // tb,),
        in_specs=[pl.BlockSpec((tb, d), lambda i: (i, 0)),
                  pl.BlockSpec((None, LANES, d), lambda i: (layer, P_RQ // LANES, 0)),
                  pl.BlockSpec((1, LANES), lambda i: (0, 0)),
                  pl.BlockSpec((1, LANES), lambda i: (0, 0))],
        out_specs=[pl.BlockSpec((tb, LANES), lambda i: (i, 0)),
                   pl.BlockSpec((LANES, tb), lambda i: (0, i))],
        out_shape=[jax.ShapeDtypeStruct((t, LANES), F32),
                   jax.ShapeDtypeStruct((LANES, t), F32)],
        compiler_params=_params("arbitrary"),
    )(h, w_in_t, alog, dtb)


def _tri_inverse_all(lows):
    n = lows[0].shape[0]
    ii = lax.broadcasted_iota(jnp.int32, (n, n), 0)
    jj = lax.broadcasted_iota(jnp.int32, (n, n), 1)
    ts = None
    for level in range(int(math.log2(n))):
        rb = jnp.right_shift(ii, level)
        cb = jnp.right_shift(jj, level)
        sel = (jnp.bitwise_and(rb, 1) == 1) & (cb == rb - 1)
        if level == 0:
            eye = jnp.where(ii == jj, 1.0, 0.0)
            ts = [eye - jnp.where(sel, low, 0.0) for low in lows]
        else:
            t16 = [t.astype(BF16) for t in ts]
            ys = [_dot(jnp.where(sel, low, 0.0).astype(BF16), t).astype(BF16) for low, t in zip(lows, t16)]
            ts = [t - _dot(tb16, y) for t, tb16, y in zip(ts, t16, ys)]
    return ts


def _gdn_kernel(q_ref, k_ref, v_ref, z_ref, bg_ref, bgt_ref, nw_ref, o_ref, s_ref):
    ck = GDN_CHUNK
    hd = GDN_HEAD_DIM
    n_chunks = q_ref.shape[0] // ck

    @pl.when(pl.program_id(0) == 0)
    def _():
        s_ref[...] = jnp.zeros_like(s_ref)

    ii = lax.broadcasted_iota(jnp.int32, (ck, ck), 0)
    jj = lax.broadcasted_iota(jnp.int32, (ck, ck), 1)
    causal = ii >= jj
    strict = ii > jj
    nw = nw_ref[...]
    heads = range(GDN_HEADS)
    items = [(c, h) for c in range(n_chunks) for h in heads]
    rows = lambda c: slice(c * ck, (c + 1) * ck)
    cols = lambda h: slice(h * hd, (h + 1) * hd)

    q = [q_ref[rows(c), cols(h)] for c, h in items]
    k = [k_ref[rows(c), cols(h)] for c, h in items]
    beta = [bg_ref[rows(c), h:h + 1] for c, h in items]
    gc = [bg_ref[rows(c), GDN_HEADS + h:GDN_HEADS + h + 1] for c, h in items]
    gl = [bg_ref[rows(c), 2 * GDN_HEADS + h:2 * GDN_HEADS + h + 1] for c, h in items]
    gc_row = [bgt_ref[h:h + 1, rows(c)] for c, h in items]
    n_items = range(len(items))
    eg = [jnp.exp(gc[n]) for n in n_items]
    decay = [jnp.exp(jnp.where(causal, gc[n] - gc_row[n], -jnp.inf)) for n in n_items]
    kb = [k[n] * beta[n] for n in n_items]
    kq = [_dot_nt(jnp.concatenate([kb[n], q[n]], axis=0).astype(BF16), k[n].astype(BF16)) for n in n_items]
    low = [jnp.where(strict, kq[n][:ck] * decay[n], 0.0) for n in n_items]
    attn = [(kq[n][ck:] * decay[n]).astype(BF16) for n in n_items]
    t_inv = _tri_inverse_all(low)
    rhs = [jnp.concatenate([v_ref[rows(c), cols(h)] * beta[n], kb[n] * eg[n]], axis=1).astype(BF16)
           for n, (c, h) in enumerate(items)]
    uw = [_dot(t_inv[n].astype(BF16), rhs[n]) for n in n_items]
    wq = [jnp.concatenate([uw[n][:, hd:], q[n] * eg[n]], axis=0).astype(BF16) for n in n_items]
    k_dec_t = [(k[n] * jnp.exp(gl[n] - gc[n])).T.astype(BF16) for n in n_items]
    egl = [jnp.broadcast_to(jnp.exp(gl[n]), (ck, hd))[0:1, :] for n in n_items]

    state = [s_ref[h] for h in heads]
    for c in range(n_chunks):
        base = c * GDN_HEADS
        ws = [_dot(wq[base + h], state[h].astype(BF16)) for h in heads]
        v_new = [(uw[base + h][:, :hd] - ws[h][:ck]).astype(BF16) for h in heads]
        state = [state[h] * egl[base + h] + _dot(k_dec_t[base + h], v_new[h]) for h in heads]
        o = [ws[h][ck:] + _dot(attn[base + h], v_new[h]) for h in heads]
        o = [o[h] * lax.rsqrt(jnp.mean(o[h] * o[h], axis=-1, keepdims=True) + NORM_EPS) * nw for h in heads]
        o_ref[rows(c), :] = jnp.concatenate(
            [(o[h] * z_ref[rows(c), cols(h)]).astype(o_ref.dtype) for h in heads], axis=1)
    for h in heads:
        s_ref[h] = state[h]


def _gdn(qk, v, z, bg, bgt, norm_w):
    t = qk.shape[0]
    tb = TB_GDN
    gd, hd = GDN_DIM, GDN_HEAD_DIM
    col = lambda c: pl.BlockSpec((tb, gd), lambda i: (i, c))
    return pl.pallas_call(
        _gdn_kernel,
        grid=(t // tb,),
        in_specs=[col(0), col(1), col(0), col(0),
                  pl.BlockSpec((tb, LANES), lambda i: (i, 0)),
                  pl.BlockSpec((SUBLANES, tb), lambda i: (1, i)),
                  pl.BlockSpec((1, hd), lambda i: (0, 0))],
        out_specs=pl.BlockSpec((tb, gd), lambda i: (i, 0)),
        out_shape=jax.ShapeDtypeStruct((t, gd), BF16),
        scratch_shapes=[pltpu.VMEM((GDN_HEADS, hd, hd), F32)],
        compiler_params=_params("arbitrary"),
    )(qk, qk, v, z, bg, bgt, norm_w.reshape(1, hd))


def _rope_kernel(pos_ref, cos_ref, sin_ref):
    half = RET_QK_DIM // 2
    lane = lax.broadcasted_iota(jnp.int32, (1, RET_QK_DIM), 1)
    idx = jnp.where(lane < half, lane, lane - half).astype(F32)
    inv_freq = jnp.exp(idx * (-math.log(ROPE_BASE) / half))
    ang = pos_ref[...].astype(F32) * inv_freq
    cos_ref[...] = jnp.cos(ang)
    sin_ref[...] = jnp.where(lane < half, -1.0, 1.0) * jnp.sin(ang)


def _rope_tables(positions):
    t = positions.shape[0]
    tb = 1024
    return pl.pallas_call(
        _rope_kernel,
        grid=(t // tb,),
        in_specs=[pl.BlockSpec((tb, 1), lambda i: (i, 0))],
        out_specs=[pl.BlockSpec((tb, RET_QK_DIM), lambda i: (i, 0))] * 2,
        out_shape=[jax.ShapeDtypeStruct((t, RET_QK_DIM), F32)] * 2,
        compiler_params=_params("arbitrary"),
    )(positions.reshape(t, 1))


def _ret_kernel(q_ref, k_ref, v_ref, g_ref, cos_ref, sin_ref, o_ref, r_ref, dmask_ref, qs_ref, ks_ref):
    tb = q_ref.shape[0]
    qk, dv = RET_QK_DIM, RET_V_DIM
    heads = range(RET_HEADS)
    log_gamma = [math.log(1.0 - 2.0 ** (-RET_DECAY_BASE - h)) for h in heads]

    @pl.when(pl.program_id(0) == 0)
    def _():
        r_ref[...] = jnp.zeros_like(r_ref)
        ii = lax.broadcasted_iota(jnp.int32, (tb, tb), 0)
        jj = lax.broadcasted_iota(jnp.int32, (tb, tb), 1)
        rel = (ii - jj).astype(F32)
        idx = lax.broadcasted_iota(jnp.int32, (tb, qk), 0).astype(F32)
        for h in heads:
            dmask_ref[h] = jnp.where(rel >= 0.0, jnp.exp(jnp.maximum(rel, 0.0) * log_gamma[h]), 0.0)
            qs_ref[h] = jnp.exp((idx + 1.0) * log_gamma[h])
            ks_ref[h] = jnp.exp((tb - 1.0 - idx) * log_gamma[h])

    cos, sin = cos_ref[...], sin_ref[...]
    rope = lambda x: x * cos + pltpu.roll(x, qk // 2, axis=1) * sin
    q = [rope(q_ref[:, h * qk:(h + 1) * qk]) for h in heads]
    k = [rope(k_ref[:, h * qk:(h + 1) * qk]) * (qk ** -0.5) for h in heads]
    v16 = [v_ref[:, h * dv:(h + 1) * dv].astype(BF16) for h in heads]
    scores = [(_dot_nt(q[h].astype(BF16), k[h].astype(BF16)) * dmask_ref[h]).astype(BF16) for h in heads]
    state = [r_ref[h] for h in heads]
    o = [_dot(scores[h], v16[h]) + _dot((q[h] * qs_ref[h]).astype(BF16), state[h].astype(BF16)) for h in heads]
    for h in heads:
        r_ref[h] = state[h] * math.exp(tb * log_gamma[h]) + _dot((k[h] * ks_ref[h]).T.astype(BF16), v16[h])
    outs = []
    for h in heads:
        oc = o[h] - jnp.mean(o[h], axis=-1, keepdims=True)
        on = oc * lax.rsqrt(jnp.mean(oc * oc, axis=-1, keepdims=True) + LN_EPS)
        outs.append((on * _silu(g_ref[:, h * dv:(h + 1) * dv])).astype(o_ref.dtype))
    o_ref[...] = jnp.concatenate(outs, axis=1)


def _retention(rest, cos, sin):
    t = rest.shape[0]
    tb = TB_RET
    qk, dv, nh = RET_QK_DIM, RET_V_DIM, RET_HEADS
    return pl.pallas_call(
        _ret_kernel,
        grid=(t // tb,),
        in_specs=[pl.BlockSpec((tb, RET_QK), lambda i: (i, R_RQ // RET_QK)),
                  pl.BlockSpec((tb, RET_QK), lambda i: (i, R_RK // RET_QK)),
                  pl.BlockSpec((tb, RET_V), lambda i: (i, R_RV // RET_V)),
                  pl.BlockSpec((tb, RET_V), lambda i: (i, R_RG // RET_V)),
                  pl.BlockSpec((tb, qk), lambda i: (i, 0)),
                  pl.BlockSpec((tb, qk), lambda i: (i, 0))],
        out_specs=pl.BlockSpec((tb, RET_V), lambda i: (i, 0)),
        out_shape=jax.ShapeDtypeStruct((t, RET_V), BF16),
        scratch_shapes=[pltpu.VMEM((nh, qk, dv), F32),
                        pltpu.VMEM((nh, tb, tb), F32),
                        pltpu.VMEM((nh, tb, qk), F32),
                        pltpu.VMEM((nh, tb, qk), F32)],
        compiler_params=_params("arbitrary"),
    )(rest, rest, rest, rest, cos, sin)


def _conformer_kernel(ca_ref, cb_ref, w_ref, b_ref, lnw_ref, lnb_ref, o_ref, upad_ref, sh_ref):
    tb = ca_ref.shape[0]
    rows = tb + HALO

    @pl.when(pl.program_id(0) == 0)
    def _():
        upad_ref[0:HALO, :] = jnp.zeros((HALO, CONV_DIM), F32)

    upad_ref[HALO:, :] = ca_ref[...] * jax.nn.sigmoid(cb_ref[...])
    full = upad_ref[...]
    for r in range(1, SUBLANES):
        sh_ref[r - 1] = pltpu.roll(full, rows - r, axis=0)
    base = HALO - (CONV_WIDTH - 1)
    acc = None
    for j in range(CONV_WIDTH):
        off = base + j
        r, a = off % SUBLANES, off - off % SUBLANES
        src = upad_ref[a:a + tb, :] if r == 0 else sh_ref[r - 1, a:a + tb, :]
        term = w_ref[j:j + 1, :] * src
        acc = term if acc is None else acc + term
    upad_ref[0:HALO, :] = upad_ref[tb:tb + HALO, :]
    u = acc + b_ref[...]
    mu = jnp.mean(u, axis=-1, keepdims=True)
    uc = u - mu
    u = uc * lax.rsqrt(jnp.mean(uc * uc, axis=-1, keepdims=True) + LN_EPS) * lnw_ref[...] + lnb_ref[...]
    o_ref[...] = _silu(u).astype(o_ref.dtype)


def _conformer(rest, w, b, ln_w, ln_b):
    t = rest.shape[0]
    tb = TB_CONV
    c = CONV_DIM
    vec = lambda: pl.BlockSpec((1, c), lambda i: (0, 0))
    return pl.pallas_call(
        _conformer_kernel,
        grid=(t // tb,),
        in_specs=[pl.BlockSpec((tb, c), lambda i: (i, R_GLU // c)),
                  pl.BlockSpec((tb, c), lambda i: (i, R_GLU // c + 1)),
                  pl.BlockSpec((CONV_WIDTH, c), lambda i: (0, 0)),
                  vec(), vec(), vec()],
        out_specs=pl.BlockSpec((tb, c), lambda i: (i, 0)),
        out_shape=jax.ShapeDtypeStruct((t, c), BF16),
        scratch_shapes=[pltpu.VMEM((tb + HALO, c), F32),
                        pltpu.VMEM((SUBLANES - 1, tb + HALO, c), F32)],
        compiler_params=_params("arbitrary"),
    )(rest, rest, w, b.reshape(1, c), ln_w.reshape(1, c), ln_b.reshape(1, c))


def _merge_kernel(oa_ref, ob_ref, oc_ref, wa_ref, wb_ref, wc_ref, ga_ref, gb_ref, gc_ref, o_ref,
                  wa16_ref, wb16_ref, wc16_ref):
    @pl.when(pl.program_id(1) == 0)
    def _():
        wa16_ref[...] = wa_ref[...].astype(BF16)
        wb16_ref[...] = wb_ref[...].astype(BF16)
        wc16_ref[...] = wc_ref[...].astype(BF16)
    m = ga_ref[...].astype(F32) * _dot(oa_ref[...], wa16_ref[...])
    m = m + gb_ref[...].astype(F32) * _dot(ob_ref[...], wb16_ref[...])
    m = m + gc_ref[...].astype(F32) * _dot(oc_ref[...], wc16_ref[...])
    o_ref[...] = m.astype(o_ref.dtype)


def _merge(o_a, o_b, o_c, w_a, w_b, w_c, layer, gates):
    t, k = o_a.shape
    n = w_a.shape[2]
    tm, tn = min(TM_MERGE, t), TN_MERGE
    act = lambda: pl.BlockSpec((tm, k), lambda j, i: (i, 0))
    wsp = lambda: pl.BlockSpec((None, k, tn), lambda j, i: (layer, 0, j))
    gate = lambda b: pl.BlockSpec((tm, tn), lambda j, i: (i, b * n // tn + j))
    return pl.pallas_call(
        _merge_kernel,
        grid=(n // tn, t // tm),
        in_specs=[act(), act(), act(), wsp(), wsp(), wsp(), gate(0), gate(1), gate(2)],
        out_specs=pl.BlockSpec((tm, tn), lambda j, i: (i, j)),
        out_shape=jax.ShapeDtypeStruct((t, n), BF16),
        scratch_shapes=[pltpu.VMEM((k, tn), BF16)] * 3,
        compiler_params=_params("arbitrary", "arbitrary"),
    )(o_a, o_b, o_c, w_a, w_b, w_c, gates, gates, gates)


def kernel(x, c, positions, w_ada, b_ada, norm_mix_w, norm_mlp_w, w_in, conv_qkv_w, gdn_a_log, gdn_dt_bias,
           gdn_norm_w, conv_dw_w, conv_dw_b, conv_ln_w, conv_ln_b, w_branch_a, w_branch_b, w_branch_c,
           w_out, w_mlp_in, w_mlp_out, final_norm_w):
    bsz, t, d = x.shape
    assert bsz == 1
    xs = x.reshape(t, d)
    mod = _adaln_mod(c, w_ada, b_ada)
    cos, sin = _rope_tables(positions.reshape(t))
    w_in_t = jnp.swapaxes(w_in, 1, 2)
    for l in range(w_in.shape[0]):
        h = _norm_mod(xs, norm_mix_w[l], mod[l], 0, 1)
        qk = _proj(h, w_in_t, l, 0, 2 * GDN_DIM, "conv_l2", conv_qkv_w[l], 0)
        v = _proj(h, w_in_t, l, 2 * GDN_DIM, GDN_DIM, "conv", conv_qkv_w[l], 2 * GDN_DIM)
        z = _proj(h, w_in_t, l, 3 * GDN_DIM, GDN_DIM, "silu")
        rest = _proj(h, w_in_t, l, P_RQ + BA_COLS, R_GATE, "plain")
        gates = _proj(h, w_in_t, l, P_RQ + BA_COLS + R_GATE, R_WIDTH - R_GATE, "sigmoid", out_dtype=BF16)
        bg, bgt = _gdn_gates(h, w_in_t, l, gdn_a_log[l], gdn_dt_bias[l])
        o_a = _gdn(qk, v, z, bg, bgt, gdn_norm_w[l])
        o_b = _retention(rest, cos, sin)
        o_c = _conformer(rest, conv_dw_w[l], conv_dw_b[l], conv_ln_w[l], conv_ln_b[l])
        merged = _merge(o_a, o_b, o_c, w_branch_a, w_branch_b, w_branch_c, l, gates)
        xs, h = _out_proj_norm(merged, w_out, l, xs, mod[l, 2:3], norm_mlp_w[l], mod[l], 3, 4)
        act = _matmul(h, w_mlp_in, l, tm=TM_WIDE, out_dtype=BF16, relu2=True)
        xs = _mlp_out(act, w_mlp_out, l, xs, mod[l, 5:6])
    return _final_norm(xs, final_norm_w).reshape(bsz, t, d)
```

```python
import functools
import math

import jax
import jax.numpy as jnp
from jax import lax
from jax.experimental import pallas as pl
from jax.experimental.pallas import tpu as pltpu

F32 = jnp.float32
BF16 = jnp.bfloat16

D_MODEL = 2048
GDN_HEADS = 8
GDN_HEAD_DIM = 128
GDN_DIM = GDN_HEADS * GDN_HEAD_DIM
SHORT_CONV = 4
RET_HEADS = 4
RET_QK_DIM = 128
RET_V_DIM = 256
RET_QK = RET_HEADS * RET_QK_DIM
RET_V = RET_HEADS * RET_V_DIM
RET_DECAY_BASE = 5.0
ROPE_BASE = 10000.0
CONV_DIM = D_MODEL // 2
CONV_WIDTH = 31
N_BRANCH = 3
NORM_EPS = 1e-6
LN_EPS = 1e-5
L2_EPS = 1e-6

BA_COLS = 2 * GDN_HEADS
P_RQ = 4 * GDN_DIM
R_RQ = 0
R_RK = R_RQ + RET_QK
R_RV = R_RK + RET_QK
R_RG = R_RV + RET_V
R_GLU = R_RG + RET_V
R_GATE = R_GLU + 2 * CONV_DIM
R_WIDTH = R_GATE + N_BRANCH * D_MODEL

LANES = 128
SUBLANES = 8
VMEM_LIMIT_BYTES = 60 * 1024 * 1024
TM = 1024
TM_WIDE = 2048
TM_PROJ = 2048
TM_MERGE = 512
TN_MERGE = 1024
TM_OUT = 512
TN = 1024
ROW_CHUNK = 256
GDN_CHUNK = 128
TB_GDN = 512
TB_GATE = 1024
TB_RET = 256
TB_CONV = 512
TB_NORM = 1024
HALO = 32


def _params(*sem):
    return pltpu.CompilerParams(dimension_semantics=sem, vmem_limit_bytes=VMEM_LIMIT_BYTES)


def _dot(a, b):
    return jnp.dot(a, b, preferred_element_type=F32)


def _dot_nt(a, b):
    return lax.dot_general(a, b, (((1,), (1,)), ((), ())), preferred_element_type=F32)


def _dot_exact_lhs(m_bf16, g):
    g1 = g.astype(BF16)
    r1 = g - g1.astype(F32)
    g2 = r1.astype(BF16)
    g3 = (r1 - g2.astype(F32)).astype(BF16)
    return _dot(m_bf16, g1) + _dot(m_bf16, g2) + _dot(m_bf16, g3)


def _silu(x):
    return x * jax.nn.sigmoid(x)


def _mod_kernel(c_ref, w_ref, b_ref, o_ref):
    prod = _silu(c_ref[...]) * w_ref[0]
    d, tn = prod.shape
    part = jnp.sum(prod.reshape(d // SUBLANES, SUBLANES, tn), axis=0)
    o_ref[0] = jnp.sum(part, axis=0, keepdims=True) + b_ref[0]


def _adaln_mod(c, w_ada, b_ada):
    depth, d, n = w_ada.shape
    tn = 1024
    out = pl.pallas_call(
        _mod_kernel,
        grid=(depth, n // tn),
        in_specs=[pl.BlockSpec((d, 1), lambda l, j: (0, 0)),
                  pl.BlockSpec((1, d, tn), lambda l, j: (l, 0, j)),
                  pl.BlockSpec((1, 1, tn), lambda l, j: (l, 0, j))],
        out_specs=pl.BlockSpec((1, 1, tn), lambda l, j: (l, 0, j)),
        out_shape=jax.ShapeDtypeStruct((depth, 1, n), F32),
        compiler_params=_params("arbitrary", "arbitrary"),
    )(c.reshape(d, 1), w_ada, b_ada.reshape(depth, 1, n))
    return out.reshape(depth, 6, d)


def _norm_mod_kernel(x_ref, w_ref, mod_ref, o_ref, *, shift_row, scale_row):
    x = x_ref[...]
    y = x * lax.rsqrt(jnp.mean(x * x, axis=-1, keepdims=True) + NORM_EPS) * w_ref[...]
    y = y * (1.0 + mod_ref[scale_row:scale_row + 1, :]) + mod_ref[shift_row:shift_row + 1, :]
    o_ref[...] = y.astype(o_ref.dtype)


def _norm_mod(x, w, mod, shift_row, scale_row):
    t, d = x.shape
    return pl.pallas_call(
        functools.partial(_norm_mod_kernel, shift_row=shift_row, scale_row=scale_row),
        grid=(t // TB_NORM,),
        in_specs=[pl.BlockSpec((TB_NORM, d), lambda i: (i, 0)),
                  pl.BlockSpec((1, d), lambda i: (0, 0)),
                  pl.BlockSpec((6, d), lambda i: (0, 0))],
        out_specs=pl.BlockSpec((TB_NORM, d), lambda i: (i, 0)),
        out_shape=jax.ShapeDtypeStruct((t, d), BF16),
        compiler_params=_params("arbitrary"),
    )(x, w.reshape(1, d), mod)


def _final_norm_kernel(x_ref, w_ref, o_ref):
    x = x_ref[...]
    o_ref[...] = x * lax.rsqrt(jnp.mean(x * x, axis=-1, keepdims=True) + NORM_EPS) * w_ref[...]


def _final_norm(x, w):
    t, d = x.shape
    return pl.pallas_call(
        _final_norm_kernel,
        grid=(t // TB_NORM,),
        in_specs=[pl.BlockSpec((TB_NORM, d), lambda i: (i, 0)),
                  pl.BlockSpec((1, d), lambda i: (0, 0))],
        out_specs=pl.BlockSpec((TB_NORM, d), lambda i: (i, 0)),
        out_shape=jax.ShapeDtypeStruct((t, d), F32),
        compiler_params=_params("arbitrary"),
    )(x, w.reshape(1, d))


def _mm_kernel(a_ref, w_ref, o_ref, wb_ref):
    @pl.when(pl.program_id(1) == 0)
    def _():
        wb_ref[...] = w_ref[...].astype(BF16)
    o_ref[...] = _dot(a_ref[...], wb_ref[...]).astype(o_ref.dtype)


def _mm_relu2_kernel(a_ref, w_ref, o_ref, wb_ref):
    @pl.when(pl.program_id(1) == 0)
    def _():
        wb_ref[...] = w_ref[...].astype(BF16)
    y = jnp.maximum(_dot(a_ref[...], wb_ref[...]), 0.0)
    o_ref[...] = (y * y).astype(o_ref.dtype)


def _matmul(a, w, layer, *, tm=TM, out_dtype=F32, relu2=False):
    t, k = a.shape
    n = w.shape[2]
    tn = min(TN, n)
    tm = min(tm, t)
    return pl.pallas_call(
        _mm_relu2_kernel if relu2 else _mm_kernel,
        grid=(n // tn, t // tm),
        in_specs=[pl.BlockSpec((tm, k), lambda j, i: (i, 0)),
                  pl.BlockSpec((None, k, tn), lambda j, i: (layer, 0, j))],
        out_specs=pl.BlockSpec((tm, tn), lambda j, i: (i, j)),
        out_shape=jax.ShapeDtypeStruct((t, n), out_dtype),
        scratch_shapes=[pltpu.VMEM((k, tn), BF16)],
        compiler_params=_params("arbitrary", "arbitrary"),
    )(a, w)


def _out_norm_kernel(a_ref, w_ref, x_ref, g_ref, nw_ref, mod_ref, xo_ref, h_ref, wb_ref, *, shift_row, scale_row):
    @pl.when(pl.program_id(0) == 0)
    def _():
        wb_ref[...] = w_ref[...].astype(BF16)
    x = x_ref[...] + g_ref[...] * _dot(a_ref[...], wb_ref[...])
    xo_ref[...] = x
    y = x * lax.rsqrt(jnp.mean(x * x, axis=-1, keepdims=True) + NORM_EPS) * nw_ref[...]
    y = y * (1.0 + mod_ref[scale_row:scale_row + 1, :]) + mod_ref[shift_row:shift_row + 1, :]
    h_ref[...] = y.astype(h_ref.dtype)


def _out_proj_norm(a, w, layer, x, gate, norm_w, mod, shift_row, scale_row):
    t, k = a.shape
    n = w.shape[2]
    tm = TM_OUT
    row = lambda: pl.BlockSpec((tm, n), lambda i: (i, 0))
    return pl.pallas_call(
        functools.partial(_out_norm_kernel, shift_row=shift_row, scale_row=scale_row),
        grid=(t // tm,),
        in_specs=[pl.BlockSpec((tm, k), lambda i: (i, 0)),
                  pl.BlockSpec((None, k, n), lambda i: (layer, 0, 0), pipeline_mode=pl.Buffered(1)),
                  row(),
                  pl.BlockSpec((1, n), lambda i: (0, 0)),
                  pl.BlockSpec((1, n), lambda i: (0, 0)),
                  pl.BlockSpec((6, n), lambda i: (0, 0))],
        out_specs=[row(), row()],
        out_shape=[jax.ShapeDtypeStruct((t, n), F32), jax.ShapeDtypeStruct((t, n), BF16)],
        scratch_shapes=[pltpu.VMEM((k, n), BF16)],
        compiler_params=_params("arbitrary"),
    )(a, w, x, gate, norm_w.reshape(1, n), mod)


def _proj_kernel(*refs, mode, shifted):
    if mode in ("conv", "conv_l2"):
        a_ref, w_ref, cw_ref, o_ref, wb_ref, carry_ref = refs
    elif shifted:
        a_ref, w_ref, wn_ref, o_ref, wb_ref = refs
    else:
        a_ref, w_ref, o_ref, wb_ref = refs
    j = pl.program_id(0)
    tm, tn = o_ref.shape

    @pl.when(pl.program_id(1) == 0)
    def _():
        if shifted:
            keep = w_ref.shape[0] - BA_COLS
            wb_ref[0:keep, :] = w_ref[BA_COLS:, :].astype(BF16)
            wb_ref[keep:, :] = wn_ref[...].astype(BF16)
        else:
            wb_ref[...] = w_ref[...].astype(BF16)
        if mode in ("conv", "conv_l2"):
            carry_ref[...] = jnp.zeros_like(carry_ref)

    wb = wb_ref[...]
    hd = GDN_HEAD_DIM
    prev = carry_ref[...] if mode in ("conv", "conv_l2") else None
    for r in range(tm // ROW_CHUNK):
        rows = slice(r * ROW_CHUNK, (r + 1) * ROW_CHUNK)
        y = _dot_nt(a_ref[rows, :], wb)
        if mode == "plain":
            o_ref[rows, :] = y
        elif mode == "silu":
            o_ref[rows, :] = _silu(y)
        elif mode == "sigmoid":
            o_ref[rows, :] = jax.nn.sigmoid(y).astype(o_ref.dtype)
        else:
            full = jnp.concatenate([prev, y], axis=0)
            acc = cw_ref[SHORT_CONV - 1:SHORT_CONV, :] * y
            for tap in range(SHORT_CONV - 1):
                acc = acc + cw_ref[tap:tap + 1, :] * pltpu.roll(full, SHORT_CONV - 1 - tap, axis=0)[SUBLANES:]
            prev = y[ROW_CHUNK - SUBLANES:, :]
            s = _silu(acc)
            if mode == "conv":
                o_ref[rows, :] = s
            else:
                scale = jnp.where(j < GDN_DIM // tn, hd ** -0.5, 1.0)
                for g in range(tn // hd):
                    blk = s[:, g * hd:(g + 1) * hd]
                    inv = lax.rsqrt(jnp.sum(blk * blk, axis=-1, keepdims=True) + L2_EPS) * scale
                    o_ref[rows, g * hd:(g + 1) * hd] = blk * inv
    if mode in ("conv", "conv_l2"):
        carry_ref[...] = prev


def _proj(a, w_in_t, layer, row0, width, mode, conv_w=None, conv_col0=0, out_dtype=F32):
    t, k = a.shape
    tn = TN
    shifted = row0 >= P_RQ
    r0 = row0 - BA_COLS if shifted else row0
    assert r0 % tn == 0 and width % tn == 0
    tm = min(TM_PROJ, t)
    in_specs = [pl.BlockSpec((tm, k), lambda j, i: (i, 0)),
                pl.BlockSpec((None, tn, k), lambda j, i: (layer, r0 // tn + j, 0))]
    args = [a, w_in_t]
    scratch = [pltpu.VMEM((tn, k), BF16)]
    if shifted:
        per = tn // BA_COLS
        in_specs.append(pl.BlockSpec((None, BA_COLS, k), lambda j, i: (layer, (r0 // tn + j + 1) * per, 0)))
        args.append(w_in_t)
    if mode in ("conv", "conv_l2"):
        in_specs.append(pl.BlockSpec((SHORT_CONV, tn), lambda j, i: (0, conv_col0 // tn + j)))
        args.append(conv_w)
        scratch.append(pltpu.VMEM((SUBLANES, tn), F32))
    return pl.pallas_call(
        functools.partial(_proj_kernel, mode=mode, shifted=shifted),
        grid=(width // tn, t // tm),
        in_specs=in_specs,
        out_specs=pl.BlockSpec((tm, tn), lambda j, i: (i, j)),
        out_shape=jax.ShapeDtypeStruct((t, width), out_dtype),
        scratch_shapes=scratch,
        compiler_params=_params("arbitrary", "arbitrary"),
    )(*args)


def _mlp_out_kernel(a_ref, w_ref, x_ref, g_ref, o_ref, acc_ref):
    kk = pl.program_id(2)

    @pl.when(kk == 0)
    def _():
        acc_ref[...] = jnp.zeros_like(acc_ref)
    acc_ref[...] += _dot(a_ref[...], w_ref[...].astype(BF16))

    @pl.when(kk == pl.num_programs(2) - 1)
    def _():
        o_ref[...] = x_ref[...] + g_ref[...] * acc_ref[...]


def _mlp_out(a, w, layer, x, gate):
    t, k = a.shape
    n = w.shape[2]
    tm, tn, tk = 1024, 1024, 2048
    return pl.pallas_call(
        _mlp_out_kernel,
        grid=(t // tm, n // tn, k // tk),
        in_specs=[pl.BlockSpec((tm, tk), lambda i, j, kk: (i, kk)),
                  pl.BlockSpec((None, tk, tn), lambda i, j, kk: (layer, kk, j)),
                  pl.BlockSpec((tm, tn), lambda i, j, kk: (i, j)),
                  pl.BlockSpec((1, tn), lambda i, j, kk: (0, j))],
        out_specs=pl.BlockSpec((tm, tn), lambda i, j, kk: (i, j)),
        out_shape=jax.ShapeDtypeStruct((t, n), F32),
        scratch_shapes=[pltpu.VMEM((tm, tn), F32)],
        compiler_params=_params("arbitrary", "arbitrary", "arbitrary"),
    )(a, w, x, gate)


def _gdn_gate_kernel(h_ref, w_ref, alog_ref, dtb_ref, bg_ref, bgt_ref):
    tb = h_ref.shape[0]
    ba = _dot_nt(h_ref[...], w_ref[...].astype(BF16))
    beta = jax.nn.sigmoid(ba)
    xs = ba + dtb_ref[...]
    softplus = jnp.maximum(xs, 0.0) + jnp.log1p(jnp.exp(-jnp.abs(xs)))
    g = -jnp.exp(alog_ref[...]) * softplus
    ck = GDN_CHUNK
    ii = lax.broadcasted_iota(jnp.int32, (ck, ck), 0)
    jj = lax.broadcasted_iota(jnp.int32, (ck, ck), 1)
    tri = jnp.where(ii >= jj, 1.0, 0.0).astype(BF16)
    ones = jnp.ones((ck, ck), BF16)
    chunks = [g[s * ck:(s + 1) * ck] for s in range(tb // ck)]
    gc = jnp.concatenate([_dot_exact_lhs(tri, gs) for gs in chunks], axis=0)
    gl = jnp.concatenate([_dot_exact_lhs(ones, gs) for gs in chunks], axis=0)
    lane = lax.broadcasted_iota(jnp.int32, (tb, LANES), 1)
    out = jnp.where(lane < GDN_HEADS, beta,
                    jnp.where(lane < 2 * GDN_HEADS, gc,
                              jnp.where(lane < 3 * GDN_HEADS, pltpu.roll(gl, GDN_HEADS, axis=1), 0.0)))
    bg_ref[...] = out
    bgt_ref[...] = out.T


def _gdn_gates(h, w_in_t, layer, a_log, dt_bias):
    t, d = h.shape
    tb = TB_GATE
    lead = jnp.zeros((GDN_HEADS,), F32)
    tail = jnp.zeros((LANES - 2 * GDN_HEADS,), F32)
    alog = jnp.concatenate([lead, a_log, tail]).reshape(1, LANES)
    dtb = jnp.concatenate([lead, dt_bias, tail]).reshape(1, LANES)
    return pl.pallas_call(
        _gdn_gate_kernel,
        grid=(t // tb,),
        in_specs=[pl.BlockSpec((tb, d), lambda i: (i, 0)),
                  pl.BlockSpec((None, LANES, d), lambda i: (layer, P_RQ // LANES, 0)),
                  pl.BlockSpec((1, LANES), lambda i: (0, 0)),
                  pl.BlockSpec((1, LANES), lambda i: (0, 0))],
        out_specs=[pl.BlockSpec((tb, LANES), lambda i: (i, 0)),
                   pl.BlockSpec((LANES, tb), lambda i: (0, i))],
        out_shape=[jax.ShapeDtypeStruct((t, LANES), F32),
                   jax.ShapeDtypeStruct((LANES, t), F32)],
        compiler_params=_params("arbitrary"),
    )(h, w_in_t, alog, dtb)


def _tri_inverse_all(lows):
    n = lows[0].shape[0]
    ii = lax.broadcasted_iota(jnp.int32, (n, n), 0)
    jj = lax.broadcasted_iota(jnp.int32, (n, n), 1)
    ts = None
    for level in range(int(math.log2(n))):
        rb = jnp.right_shift(ii, level)
        cb = jnp.right_shift(jj, level)
        sel = (jnp.bitwise_and(rb, 1) == 1) & (cb == rb - 1)
        if level == 0:
            eye = jnp.where(ii == jj, 1.0, 0.0)
            ts = [eye - jnp.where(sel, low, 0.0) for low in lows]
        else:
            t16 = [t.astype(BF16) for t in ts]
            ys = [_dot(jnp.where(sel, low, 0.0).astype(BF16), t).astype(BF16) for low, t in zip(lows, t16)]
            ts = [t - _dot(tb16, y) for t, tb16, y in zip(ts, t16, ys)]
    return ts


def _gdn_kernel(q_ref, k_ref, v_ref, z_ref, bg_ref, bgt_ref, nw_ref, o_ref, s_ref):
    ck = GDN_CHUNK
    hd = GDN_HEAD_DIM
    n_chunks = q_ref.shape[0] // ck

    @pl.when(pl.program_id(0) == 0)
    def _():
        s_ref[...] = jnp.zeros_like(s_ref)

    ii = lax.broadcasted_iota(jnp.int32, (ck, ck), 0)
    jj = lax.broadcasted_iota(jnp.int32, (ck, ck), 1)
    causal = ii >= jj
    strict = ii > jj
    nw = nw_ref[...]
    heads = range(GDN_HEADS)
    items = [(c, h) for c in range(n_chunks) for h in heads]
    rows = lambda c: slice(c * ck, (c + 1) * ck)
    cols = lambda h: slice(h * hd, (h + 1) * hd)

    q = [q_ref[rows(c), cols(h)] for c, h in items]
    k = [k_ref[rows(c), cols(h)] for c, h in items]
    beta = [bg_ref[rows(c), h:h + 1] for c, h in items]
    gc = [bg_ref[rows(c), GDN_HEADS + h:GDN_HEADS + h + 1] for c, h in items]
    gl = [bg_ref[rows(c), 2 * GDN_HEADS + h:2 * GDN_HEADS + h + 1] for c, h in items]
    gc_row = [bgt_ref[h:h + 1, rows(c)] for c, h in items]
    n_items = range(len(items))
    eg = [jnp.exp(gc[n]) for n in n_items]
    decay = [jnp.exp(jnp.where(causal, gc[n] - gc_row[n], -jnp.inf)) for n in n_items]
    kb = [k[n] * beta[n] for n in n_items]
    kq = [_dot_nt(jnp.concatenate([kb[n], q[n]], axis=0).astype(BF16), k[n].astype(BF16)) for n in n_items]
    low = [jnp.where(strict, kq[n][:ck] * decay[n], 0.0) for n in n_items]
    attn = [(kq[n][ck:] * decay[n]).astype(BF16) for n in n_items]
    t_inv = _tri_inverse_all(low)
    rhs = [jnp.concatenate([v_ref[rows(c), cols(h)] * beta[n], kb[n] * eg[n]], axis=1).astype(BF16)
           for n, (c, h) in enumerate(items)]
    uw = [_dot(t_inv[n].astype(BF16), rhs[n]) for n in n_items]
    wq = [jnp.concatenate([uw[n][:, hd:], q[n] * eg[n]], axis=0).astype(BF16) for n in n_items]
    k_dec_t = [(k[n] * jnp.exp(gl[n] - gc[n])).T.astype(BF16) for n in n_items]
    egl = [jnp.broadcast_to(jnp.exp(gl[n]), (ck, hd))[0:1, :] for n in n_items]

    state = [s_ref[h] for h in heads]
    for c in range(n_chunks):
        base = c * GDN_HEADS
        ws = [_dot(wq[base + h], state[h].astype(BF16)) for h in heads]
        v_new = [(uw[base + h][:, :hd] - ws[h][:ck]).astype(BF16) for h in heads]
        state = [state[h] * egl[base + h] + _dot(k_dec_t[base + h], v_new[h]) for h in heads]
        o = [ws[h][ck:] + _dot(attn[base + h], v_new[h]) for h in heads]
        o = [o[h] * lax.rsqrt(jnp.mean(o[h] * o[h], axis=-1, keepdims=True) + NORM_EPS) * nw for h in heads]
        o_ref[rows(c), :] = jnp.concatenate(
            [(o[h] * z_ref[rows(c), cols(h)]).astype(o_ref.dtype) for h in heads], axis=1)
    for h in heads:
        s_ref[h] = state[h]


def _gdn(qk, v, z, bg, bgt, norm_w):
    t = qk.shape[0]
    tb = TB_GDN
    gd, hd = GDN_DIM, GDN_HEAD_DIM
    col = lambda c: pl.BlockSpec((tb, gd), lambda i: (i, c))
    return pl.pallas_call(
        _gdn_kernel,
        grid=(t // tb,),
        in_specs=[col(0), col(1), col(0), col(0),
                  pl.BlockSpec((tb, LANES), lambda i: (i, 0)),
                  pl.BlockSpec((SUBLANES, tb), lambda i: (1, i)),
                  pl.BlockSpec((1, hd), lambda i: (0, 0))],
        out_specs=pl.BlockSpec((tb, gd), lambda i: (i, 0)),
        out_shape=jax.ShapeDtypeStruct((t, gd), BF16),
        scratch_shapes=[pltpu.VMEM((GDN_HEADS, hd, hd), F32)],
        compiler_params=_params("arbitrary"),
    )(qk, qk, v, z, bg, bgt, norm_w.reshape(1, hd))


def _rope_kernel(pos_ref, cos_ref, sin_ref):
    half = RET_QK_DIM // 2
    lane = lax.broadcasted_iota(jnp.int32, (1, RET_QK_DIM), 1)
    idx = jnp.where(lane < half, lane, lane - half).astype(F32)
    inv_freq = jnp.exp(idx * (-math.log(ROPE_BASE) / half))
    ang = pos_ref[...].astype(F32) * inv_freq
    cos_ref[...] = jnp.cos(ang)
    sin_ref[...] = jnp.where(lane < half, -1.0, 1.0) * jnp.sin(ang)


def _rope_tables(positions):
    t = positions.shape[0]
    tb = 1024
    return pl.pallas_call(
        _rope_kernel,
        grid=(t // tb,),
        in_specs=[pl.BlockSpec((tb, 1), lambda i: (i, 0))],
        out_specs=[pl.BlockSpec((tb, RET_QK_DIM), lambda i: (i, 0))] * 2,
        out_shape=[jax.ShapeDtypeStruct((t, RET_QK_DIM), F32)] * 2,
        compiler_params=_params("arbitrary"),
    )(positions.reshape(t, 1))


def _ret_kernel(q_ref, k_ref, v_ref, g_ref, cos_ref, sin_ref, o_ref, r_ref, dmask_ref, qs_ref, ks_ref):
    tb = q_ref.shape[0]
    qk, dv = RET_QK_DIM, RET_V_DIM
    heads = range(RET_HEADS)
    log_gamma = [math.log(1.0 - 2.0 ** (-RET_DECAY_BASE - h)) for h in heads]

    @pl.when(pl.program_id(0) == 0)
    def _():
        r_ref[...] = jnp.zeros_like(r_ref)
        ii = lax.broadcasted_iota(jnp.int32, (tb, tb), 0)
        jj = lax.broadcasted_iota(jnp.int32, (tb, tb), 1)
        rel = (ii - jj).astype(F32)
        idx = lax.broadcasted_iota(jnp.int32, (tb, qk), 0).astype(F32)
        for h in heads:
            dmask_ref[h] = jnp.where(rel >= 0.0, jnp.exp(jnp.maximum(rel, 0.0) * log_gamma[h]), 0.0)
            qs_ref[h] = jnp.exp((idx + 1.0) * log_gamma[h])
            ks_ref[h] = jnp.exp((tb - 1.0 - idx) * log_gamma[h])

    cos, sin = cos_ref[...], sin_ref[...]
    rope = lambda x: x * cos + pltpu.roll(x, qk // 2, axis=1) * sin
    q = [rope(q_ref[:, h * qk:(h + 1) * qk]) for h in heads]
    k = [rope(k_ref[:, h * qk:(h + 1) * qk]) * (qk ** -0.5) for h in heads]
    v16 = [v_ref[:, h * dv:(h + 1) * dv].astype(BF16) for h in heads]
    scores = [(_dot_nt(q[h].astype(BF16), k[h].astype(BF16)) * dmask_ref[h]).astype(BF16) for h in heads]
    state = [r_ref[h] for h in heads]
    o = [_dot(scores[h], v16[h]) + _dot((q[h] * qs_ref[h]).astype(BF16), state[h].astype(BF16)) for h in heads]
    for h in heads:
        r_ref[h] = state[h] * math.exp(tb * log_gamma[h]) + _dot((k[h] * ks_ref[h]).T.astype(BF16), v16[h])
    outs = []
    for h in heads:
        oc = o[h] - jnp.mean(o[h], axis=-1, keepdims=True)
        on = oc * lax.rsqrt(jnp.mean(oc * oc, axis=-1, keepdims=True) + LN_EPS)
        outs.append((on * _silu(g_ref[:, h * dv:(h + 1) * dv])).astype(o_ref.dtype))
    o_ref[...] = jnp.concatenate(outs, axis=1)


def _retention(rest, cos, sin):
    t = rest.shape[0]
    tb = TB_RET
    qk, dv, nh = RET_QK_DIM, RET_V_DIM, RET_HEADS
    return pl.pallas_call(
        _ret_kernel,
        grid=(t // tb,),
        in_specs=[pl.BlockSpec((tb, RET_QK), lambda i: (i, R_RQ // RET_QK)),
                  pl.BlockSpec((tb, RET_QK), lambda i: (i, R_RK // RET_QK)),
                  pl.BlockSpec((tb, RET_V), lambda i: (i, R_RV // RET_V)),
                  pl.BlockSpec((tb, RET_V), lambda i: (i, R_RG // RET_V)),
                  pl.BlockSpec((tb, qk), lambda i: (i, 0)),
                  pl.BlockSpec((tb, qk), lambda i: (i, 0))],
        out_specs=pl.BlockSpec((tb, RET_V), lambda i: (i, 0)),
        out_shape=jax.ShapeDtypeStruct((t, RET_V), BF16),
        scratch_shapes=[pltpu.VMEM((nh, qk, dv), F32),
                        pltpu.VMEM((nh, tb, tb), F32),
                        pltpu.VMEM((nh, tb, qk), F32),
                        pltpu.VMEM((nh, tb, qk), F32)],
        compiler_params=_params("arbitrary"),
    )(rest, rest, rest, rest, cos, sin)


def _conformer_kernel(ca_ref, cb_ref, w_ref, b_ref, lnw_ref, lnb_ref, o_ref, upad_ref, sh_ref):
    tb = ca_ref.shape[0]
    rows = tb + HALO

    @pl.when(pl.program_id(0) == 0)
    def _():
        upad_ref[0:HALO, :] = jnp.zeros((HALO, CONV_DIM), F32)

    upad_ref[HALO:, :] = ca_ref[...] * jax.nn.sigmoid(cb_ref[...])
    full = upad_ref[...]
    for r in range(1, SUBLANES):
        sh_ref[r - 1] = pltpu.roll(full, rows - r, axis=0)
    base = HALO - (CONV_WIDTH - 1)
    acc = None
    for j in range(CONV_WIDTH):
        off = base + j
        r, a = off % SUBLANES, off - off % SUBLANES
        src = upad_ref[a:a + tb, :] if r == 0 else sh_ref[r - 1, a:a + tb, :]
        term = w_ref[j:j + 1, :] * src
        acc = term if acc is None else acc + term
    upad_ref[0:HALO, :] = upad_ref[tb:tb + HALO, :]
    u = acc + b_ref[...]
    mu = jnp.mean(u, axis=-1, keepdims=True)
    uc = u - mu
    u = uc * lax.rsqrt(jnp.mean(uc * uc, axis=-1, keepdims=True) + LN_EPS) * lnw_ref[...] + lnb_ref[...]
    o_ref[...] = _silu(u).astype(o_ref.dtype)


def _conformer(rest, w, b, ln_w, ln_b):
    t = rest.shape[0]
    tb = TB_CONV
    c = CONV_DIM
    vec = lambda: pl.BlockSpec((1, c), lambda i: (0, 0))
    return pl.pallas_call(
        _conformer_kernel,
        grid=(t // tb,),
        in_specs=[pl.BlockSpec((tb, c), lambda i: (i, R_GLU // c)),
                  pl.BlockSpec((tb, c), lambda i: (i, R_GLU // c + 1)),
                  pl.BlockSpec((CONV_WIDTH, c), lambda i: (0, 0)),
                  vec(), vec(), vec()],
        out_specs=pl.BlockSpec((tb, c), lambda i: (i, 0)),
        out_shape=jax.ShapeDtypeStruct((t, c), BF16),
        scratch_shapes=[pltpu.VMEM((tb + HALO, c), F32),
                        pltpu.VMEM((SUBLANES - 1, tb + HALO, c), F32)],
        compiler_params=_params("arbitrary"),
    )(rest, rest, w, b.reshape(1, c), ln_w.reshape(1, c), ln_b.reshape(1, c))


def _merge_kernel(oa_ref, ob_ref, oc_ref, wa_ref, wb_ref, wc_ref, ga_ref, gb_ref, gc_ref, o_ref,
                  wa16_ref, wb16_ref, wc16_ref):
    @pl.when(pl.program_id(1) == 0)
    def _():
        wa16_ref[...] = wa_ref[...].astype(BF16)
        wb16_ref[...] = wb_ref[...].astype(BF16)
        wc16_ref[...] = wc_ref[...].astype(BF16)
    m = ga_ref[...].astype(F32) * _dot(oa_ref[...], wa16_ref[...])
    m = m + gb_ref[...].astype(F32) * _dot(ob_ref[...], wb16_ref[...])
    m = m + gc_ref[...].astype(F32) * _dot(oc_ref[...], wc16_ref[...])
    o_ref[...] = m.astype(o_ref.dtype)


def _merge(o_a, o_b, o_c, w_a, w_b, w_c, layer, gates):
    t, k = o_a.shape
    n = w_a.shape[2]
    tm, tn = min(TM_MERGE, t), TN_MERGE
    act = lambda: pl.BlockSpec((tm, k), lambda j, i: (i, 0))
    wsp = lambda: pl.BlockSpec((None, k, tn), lambda j, i: (layer, 0, j))
    gate = lambda b: pl.BlockSpec((tm, tn), lambda j, i: (i, b * n // tn + j))
    return pl.pallas_call(
        _merge_kernel,
        grid=(n // tn, t // tm),
        in_specs=[act(), act(), act(), wsp(), wsp(), wsp(), gate(0), gate(1), gate(2)],
        out_specs=pl.BlockSpec((tm, tn), lambda j, i: (i, j)),
        out_shape=jax.ShapeDtypeStruct((t, n), BF16),
        scratch_shapes=[pltpu.VMEM((k, tn), BF16)] * 3,
        compiler_params=_params("arbitrary", "arbitrary"),
    )(o_a, o_b, o_c, w_a, w_b, w_c, gates, gates, gates)


def kernel(x, c, positions, w_ada, b_ada, norm_mix_w, norm_mlp_w, w_in, conv_qkv_w, gdn_a_log, gdn_dt_bias,
           gdn_norm_w, conv_dw_w, conv_dw_b, conv_ln_w, conv_ln_b, w_branch_a, w_branch_b, w_branch_c,
           w_out, w_mlp_in, w_mlp_out, final_norm_w):
    bsz, t, d = x.shape
    assert bsz == 1
    xs = x.reshape(t, d)
    mod = _adaln_mod(c, w_ada, b_ada)
    cos, sin = _rope_tables(positions.reshape(t))
    w_in_t = jnp.swapaxes(w_in, 1, 2)
    for l in range(w_in.shape[0]):
        h = _norm_mod(xs, norm_mix_w[l], mod[l], 0, 1)
        qk = _proj(h, w_in_t, l, 0, 2 * GDN_DIM, "conv_l2", conv_qkv_w[l], 0)
        v = _proj(h, w_in_t, l, 2 * GDN_DIM, GDN_DIM, "conv", conv_qkv_w[l], 2 * GDN_DIM)
        z = _proj(h, w_in_t, l, 3 * GDN_DIM, GDN_DIM, "silu")
        rest = _proj(h, w_in_t, l, P_RQ + BA_COLS, R_GATE, "plain")
        gates = _proj(h, w_in_t, l, P_RQ + BA_COLS + R_GATE, R_WIDTH - R_GATE, "sigmoid", out_dtype=BF16)
        bg, bgt = _gdn_gates(h, w_in_t, l, gdn_a_log[l], gdn_dt_bias[l])
        o_a = _gdn(qk, v, z, bg, bgt, gdn_norm_w[l])
        o_b = _retention(rest, cos, sin)
        o_c = _conformer(rest, conv_dw_w[l], conv_dw_b[l], conv_ln_w[l], conv_ln_b[l])
        merged = _merge(o_a, o_b, o_c, w_branch_a, w_branch_b, w_branch_c, l, gates)
        xs, h = _out_proj_norm(merged, w_out, l, xs, mod[l, 2:3], norm_mlp_w[l], mod[l], 3, 4)
        act = _matmul(h, w_mlp_in, l, tm=TM_WIDE, out_dtype=BF16, relu2=True)
        xs = _mlp_out(act, w_mlp_out, l, xs, mod[l, 5:6])
    return _final_norm(xs, final_norm_w).reshape(bsz, t, d)
```

```python
import functools
import math

import jax
import jax.numpy as jnp
from jax import lax
from jax.experimental import pallas as pl
from jax.experimental.pallas import tpu as pltpu

F32 = jnp.float32
BF16 = jnp.bfloat16

D_MODEL = 2048
GDN_HEADS = 8
GDN_HEAD_DIM = 128
GDN_DIM = GDN_HEADS * GDN_HEAD_DIM
SHORT_CONV = 4
RET_HEADS = 4
RET_QK_DIM = 128
RET_V_DIM = 256
RET_QK = RET_HEADS * RET_QK_DIM
RET_V = RET_HEADS * RET_V_DIM
RET_DECAY_BASE = 5.0
ROPE_BASE = 10000.0
CONV_DIM = D_MODEL // 2
CONV_WIDTH = 31
N_BRANCH = 3
NORM_EPS = 1e-6
LN_EPS = 1e-5
L2_EPS = 1e-6

BA_COLS = 2 * GDN_HEADS
P_RQ = 4 * GDN_DIM
R_RQ = 0
R_RK = R_RQ + RET_QK
R_RV = R_RK + RET_QK
R_RG = R_RV + RET_V
R_GLU = R_RG + RET_V
R_GATE = R_GLU + 2 * CONV_DIM
R_WIDTH = R_GATE + N_BRANCH * D_MODEL

LANES = 128
SUBLANES = 8
VMEM_LIMIT_BYTES = 60 * 1024 * 1024
TM = 1024
TM_WIDE = 2048
TM_PROJ = 2048
TM_MERGE = 512
TN_MERGE = 1024
TM_OUT = 512
TN = 1024
ROW_CHUNK = 256
GDN_CHUNK = 128
TB_GDN = 512
TB_GATE = 1024
TB_RET = 256
TB_CONV = 512
TB_NORM = 1024
HALO = 32


def _params(*sem):
    return pltpu.CompilerParams(dimension_semantics=sem, vmem_limit_bytes=VMEM_LIMIT_BYTES)


def _dot(a, b):
    return jnp.dot(a, b, preferred_element_type=F32)


def _dot_nt(a, b):
    return lax.dot_general(a, b, (((1,), (1,)), ((), ())), preferred_element_type=F32)


def _dot_exact_lhs(m_bf16, g):
    g1 = g.astype(BF16)
    r1 = g - g1.astype(F32)
    g2 = r1.astype(BF16)
    g3 = (r1 - g2.astype(F32)).astype(BF16)
    return _dot(m_bf16, g1) + _dot(m_bf16, g2) + _dot(m_bf16, g3)


def _silu(x):
    return x * jax.nn.sigmoid(x)


def _mod_kernel(c_ref, w_ref, b_ref, o_ref):
    prod = _silu(c_ref[...]) * w_ref[0]
    d, tn = prod.shape
    part = jnp.sum(prod.reshape(d // SUBLANES, SUBLANES, tn), axis=0)
    o_ref[0] = jnp.sum(part, axis=0, keepdims=True) + b_ref[0]


def _adaln_mod(c, w_ada, b_ada):
    depth, d, n = w_ada.shape
    tn = 1024
    out = pl.pallas_call(
        _mod_kernel,
        grid=(depth, n // tn),
        in_specs=[pl.BlockSpec((d, 1), lambda l, j: (0, 0)),
                  pl.BlockSpec((1, d, tn), lambda l, j: (l, 0, j)),
                  pl.BlockSpec((1, 1, tn), lambda l, j: (l, 0, j))],
        out_specs=pl.BlockSpec((1, 1, tn), lambda l, j: (l, 0, j)),
        out_shape=jax.ShapeDtypeStruct((depth, 1, n), F32),
        compiler_params=_params("arbitrary", "arbitrary"),
    )(c.reshape(d, 1), w_ada, b_ada.reshape(depth, 1, n))
    return out.reshape(depth, 6, d)


def _norm_mod_kernel(x_ref, w_ref, mod_ref, o_ref, *, shift_row, scale_row):
    x = x_ref[...]
    y = x * lax.rsqrt(jnp.mean(x * x, axis=-1, keepdims=True) + NORM_EPS) * w_ref[...]
    y = y * (1.0 + mod_ref[scale_row:scale_row + 1, :]) + mod_ref[shift_row:shift_row + 1, :]
    o_ref[...] = y.astype(o_ref.dtype)


def _norm_mod(x, w, mod, shift_row, scale_row):
    t, d = x.shape
    return pl.pallas_call(
        functools.partial(_norm_mod_kernel, shift_row=shift_row, scale_row=scale_row),
        grid=(t // TB_NORM,),
        in_specs=[pl.BlockSpec((TB_NORM, d), lambda i: (i, 0)),
                  pl.BlockSpec((1, d), lambda i: (0, 0)),
                  pl.BlockSpec((6, d), lambda i: (0, 0))],
        out_specs=pl.BlockSpec((TB_NORM, d), lambda i: (i, 0)),
        out_shape=jax.ShapeDtypeStruct((t, d), BF16),
        compiler_params=_params("arbitrary"),
    )(x, w.reshape(1, d), mod)


def _final_norm_kernel(x_ref, w_ref, o_ref):
    x = x_ref[...]
    o_ref[...] = x * lax.rsqrt(jnp.mean(x * x, axis=-1, keepdims=True) + NORM_EPS) * w_ref[...]


def _final_norm(x, w):
    t, d = x.shape
    return pl.pallas_call(
        _final_norm_kernel,
        grid=(t // TB_NORM,),
        in_specs=[pl.BlockSpec((TB_NORM, d), lambda i: (i, 0)),
                  pl.BlockSpec((1, d), lambda i: (0, 0))],
        out_specs=pl.BlockSpec((TB_NORM, d), lambda i: (i, 0)),
        out_shape=jax.ShapeDtypeStruct((t, d), F32),
        compiler_params=_params("arbitrary"),
    )(x, w.reshape(1, d))


def _mm_kernel(a_ref, w_ref, o_ref, wb_ref):
    @pl.when(pl.program_id(1) == 0)
    def _():
        wb_ref[...] = w_ref[...].astype(BF16)
    o_ref[...] = _dot(a_ref[...], wb_ref[...]).astype(o_ref.dtype)


def _mm_relu2_kernel(a_ref, w_ref, o_ref, wb_ref):
    @pl.when(pl.program_id(1) == 0)
    def _():
        wb_ref[...] = w_ref[...].astype(BF16)
    y = jnp.maximum(_dot(a_ref[...], wb_ref[...]), 0.0)
    o_ref[...] = (y * y).astype(o_ref.dtype)


def _matmul(a, w, layer, *, tm=TM, out_dtype=F32, relu2=False):
    t, k = a.shape
    n = w.shape[2]
    tn = min(TN, n)
    tm = min(tm, t)
    return pl.pallas_call(
        _mm_relu2_kernel if relu2 else _mm_kernel,
        grid=(n // tn, t // tm),
        in_specs=[pl.BlockSpec((tm, k), lambda j, i: (i, 0)),
                  pl.BlockSpec((None, k, tn), lambda j, i: (layer, 0, j))],
        out_specs=pl.BlockSpec((tm, tn), lambda j, i: (i, j)),
        out_shape=jax.ShapeDtypeStruct((t, n), out_dtype),
        scratch_shapes=[pltpu.VMEM((k, tn), BF16)],
        compiler_params=_params("arbitrary", "arbitrary"),
    )(a, w)


def _out_norm_kernel(a_ref, w_ref, x_ref, g_ref, nw_ref, mod_ref, xo_ref, h_ref, wb_ref, *, shift_row, scale_row):
    @pl.when(pl.program_id(0) == 0)
    def _():
        wb_ref[...] = w_ref[...].astype(BF16)
    x = x_ref[...] + g_ref[...] * _dot(a_ref[...], wb_ref[...])
    xo_ref[...] = x
    y = x * lax.rsqrt(jnp.mean(x * x, axis=-1, keepdims=True) + NORM_EPS) * nw_ref[...]
    y = y * (1.0 + mod_ref[scale_row:scale_row + 1, :]) + mod_ref[shift_row:shift_row + 1, :]
    h_ref[...] = y.astype(h_ref.dtype)


def _out_proj_norm(a, w, layer, x, gate, norm_w, mod, shift_row, scale_row):
    t, k = a.shape
    n = w.shape[2]
    tm = TM_OUT
    row = lambda: pl.BlockSpec((tm, n), lambda i: (i, 0))
    return pl.pallas_call(
        functools.partial(_out_norm_kernel, shift_row=shift_row, scale_row=scale_row),
        grid=(t // tm,),
        in_specs=[pl.BlockSpec((tm, k), lambda i: (i, 0)),
                  pl.BlockSpec((None, k, n), lambda i: (layer, 0, 0), pipeline_mode=pl.Buffered(1)),
                  row(),
                  pl.BlockSpec((1, n), lambda i: (0, 0)),
                  pl.BlockSpec((1, n), lambda i: (0, 0)),
                  pl.BlockSpec((6, n), lambda i: (0, 0))],
        out_specs=[row(), row()],
        out_shape=[jax.ShapeDtypeStruct((t, n), F32), jax.ShapeDtypeStruct((t, n), BF16)],
        scratch_shapes=[pltpu.VMEM((k, n), BF16)],
        compiler_params=_params("arbitrary"),
    )(a, w, x, gate, norm_w.reshape(1, n), mod)


def _proj_kernel(*refs, mode, shifted):
    if mode in ("conv", "conv_l2"):
        a_ref, w_ref, cw_ref, o_ref, wb_ref, carry_ref = refs
    elif shifted:
        a_ref, w_ref, wn_ref, o_ref, wb_ref = refs
    else:
        a_ref, w_ref, o_ref, wb_ref = refs
    j = pl.program_id(0)
    tm, tn = o_ref.shape

    @pl.when(pl.program_id(1) == 0)
    def _():
        if shifted:
            keep = w_ref.shape[0] - BA_COLS
            wb_ref[0:keep, :] = w_ref[BA_COLS:, :].astype(BF16)
            wb_ref[keep:, :] = wn_ref[...].astype(BF16)
        else:
            wb_ref[...] = w_ref[...].astype(BF16)
        if mode in ("conv", "conv_l2"):
            carry_ref[...] = jnp.zeros_like(carry_ref)

    wb = wb_ref[...]
    hd = GDN_HEAD_DIM
    prev = carry_ref[...] if mode in ("conv", "conv_l2") else None
    for r in range(tm // ROW_CHUNK):
        rows = slice(r * ROW_CHUNK, (r + 1) * ROW_CHUNK)
        y = _dot_nt(a_ref[rows, :], wb)
        if mode == "plain":
            o_ref[rows, :] = y
        elif mode == "silu":
            o_ref[rows, :] = _silu(y)
        elif mode == "sigmoid":
            o_ref[rows, :] = jax.nn.sigmoid(y).astype(o_ref.dtype)
        else:
            full = jnp.concatenate([prev, y], axis=0)
            acc = cw_ref[SHORT_CONV - 1:SHORT_CONV, :] * y
            for tap in range(SHORT_CONV - 1):
                acc = acc + cw_ref[tap:tap + 1, :] * pltpu.roll(full, SHORT_CONV - 1 - tap, axis=0)[SUBLANES:]
            prev = y[ROW_CHUNK - SUBLANES:, :]
            s = _silu(acc)
            if mode == "conv":
                o_ref[rows, :] = s
            else:
                scale = jnp.where(j < GDN_DIM // tn, hd ** -0.5, 1.0)
                is_v = j >= 2 * GDN_DIM // tn
                for g in range(tn // hd):
                    blk = s[:, g * hd:(g + 1) * hd]
                    inv = lax.rsqrt(jnp.sum(blk * blk, axis=-1, keepdims=True) + L2_EPS) * scale
                    o_ref[rows, g * hd:(g + 1) * hd] = blk * jnp.where(is_v, 1.0, inv)
    if mode in ("conv", "conv_l2"):
        carry_ref[...] = prev


def _proj(a, w_in_t, layer, row0, width, mode, conv_w=None, conv_col0=0, out_dtype=F32):
    t, k = a.shape
    tn = TN
    shifted = row0 >= P_RQ
    r0 = row0 - BA_COLS if shifted else row0
    assert r0 % tn == 0 and width % tn == 0
    tm = min(TM_PROJ, t)
    in_specs = [pl.BlockSpec((tm, k), lambda j, i: (i, 0)),
                pl.BlockSpec((None, tn, k), lambda j, i: (layer, r0 // tn + j, 0))]
    args = [a, w_in_t]
    scratch = [pltpu.VMEM((tn, k), BF16)]
    if shifted:
        per = tn // BA_COLS
        in_specs.append(pl.BlockSpec((None, BA_COLS, k), lambda j, i: (layer, (r0 // tn + j + 1) * per, 0)))
        args.append(w_in_t)
    if mode in ("conv", "conv_l2"):
        in_specs.append(pl.BlockSpec((SHORT_CONV, tn), lambda j, i: (0, conv_col0 // tn + j)))
        args.append(conv_w)
        scratch.append(pltpu.VMEM((SUBLANES, tn), F32))
    return pl.pallas_call(
        functools.partial(_proj_kernel, mode=mode, shifted=shifted),
        grid=(width // tn, t // tm),
        in_specs=in_specs,
        out_specs=pl.BlockSpec((tm, tn), lambda j, i: (i, j)),
        out_shape=jax.ShapeDtypeStruct((t, width), out_dtype),
        scratch_shapes=scratch,
        compiler_params=_params("arbitrary", "arbitrary"),
    )(*args)


def _mlp_out_kernel(a_ref, w_ref, x_ref, g_ref, o_ref, acc_ref):
    kk = pl.program_id(2)

    @pl.when(kk == 0)
    def _():
        acc_ref[...] = jnp.zeros_like(acc_ref)
    acc_ref[...] += _dot(a_ref[...], w_ref[...].astype(BF16))

    @pl.when(kk == pl.num_programs(2) - 1)
    def _():
        o_ref[...] = x_ref[...] + g_ref[...] * acc_ref[...]


def _mlp_out(a, w, layer, x, gate):
    t, k = a.shape
    n = w.shape[2]
    tm, tn, tk = 1024, 1024, 2048
    return pl.pallas_call(
        _mlp_out_kernel,
        grid=(t // tm, n // tn, k // tk),
        in_specs=[pl.BlockSpec((tm, tk), lambda i, j, kk: (i, kk)),
                  pl.BlockSpec((None, tk, tn), lambda i, j, kk: (layer, kk, j)),
                  pl.BlockSpec((tm, tn), lambda i, j, kk: (i, j)),
                  pl.BlockSpec((1, tn), lambda i, j, kk: (0, j))],
        out_specs=pl.BlockSpec((tm, tn), lambda i, j, kk: (i, j)),
        out_shape=jax.ShapeDtypeStruct((t, n), F32),
        scratch_shapes=[pltpu.VMEM((tm, tn), F32)],
        compiler_params=_params("arbitrary", "arbitrary", "arbitrary"),
    )(a, w, x, gate)


def _gdn_gate_kernel(h_ref, w_ref, alog_ref, dtb_ref, bg_ref, bgt_ref):
    tb = h_ref.shape[0]
    ba = _dot_nt(h_ref[...], w_ref[...].astype(BF16))
    beta = jax.nn.sigmoid(ba)
    xs = ba + dtb_ref[...]
    softplus = jnp.maximum(xs, 0.0) + jnp.log1p(jnp.exp(-jnp.abs(xs)))
    g = -jnp.exp(alog_ref[...]) * softplus
    ck = GDN_CHUNK
    ii = lax.broadcasted_iota(jnp.int32, (ck, ck), 0)
    jj = lax.broadcasted_iota(jnp.int32, (ck, ck), 1)
    tri = jnp.where(ii >= jj, 1.0, 0.0).astype(BF16)
    ones = jnp.ones((ck, ck), BF16)
    chunks = [g[s * ck:(s + 1) * ck] for s in range(tb // ck)]
    gc = jnp.concatenate([_dot_exact_lhs(tri, gs) for gs in chunks], axis=0)
    gl = jnp.concatenate([_dot_exact_lhs(ones, gs) for gs in chunks], axis=0)
    lane = lax.broadcasted_iota(jnp.int32, (tb, LANES), 1)
    out = jnp.where(lane < GDN_HEADS, beta,
                    jnp.where(lane < 2 * GDN_HEADS, gc,
                              jnp.where(lane < 3 * GDN_HEADS, pltpu.roll(gl, GDN_HEADS, axis=1), 0.0)))
    bg_ref[...] = out
    bgt_ref[...] = out.T


def _gdn_gates(h, w_in_t, layer, a_log, dt_bias):
    t, d = h.shape
    tb = TB_GATE
    lead = jnp.zeros((GDN_HEADS,), F32)
    tail = jnp.zeros((LANES - 2 * GDN_HEADS,), F32)
    alog = jnp.concatenate([lead, a_log, tail]).reshape(1, LANES)
    dtb = jnp.concatenate([lead, dt_bias, tail]).reshape(1, LANES)
    return pl.pallas_call(
        _gdn_gate_kernel,
        grid=(t // tb,),
        in_specs=[pl.BlockSpec((tb, d), lambda i: (i, 0)),
                  pl.BlockSpec((None, LANES, d), lambda i: (layer, P_RQ // LANES, 0)),
                  pl.BlockSpec((1, LANES), lambda i: (0, 0)),
                  pl.BlockSpec((1, LANES), lambda i: (0, 0))],
        out_specs=[pl.BlockSpec((tb, LANES), lambda i: (i, 0)),
                   pl.BlockSpec((LANES, tb), lambda i: (0, i))],
        out_shape=[jax.ShapeDtypeStruct((t, LANES), F32),
                   jax.ShapeDtypeStruct((LANES, t), F32)],
        compiler_params=_params("arbitrary"),
    )(h, w_in_t, alog, dtb)


def _tri_inverse_all(lows):
    n = lows[0].shape[0]
    ii = lax.broadcasted_iota(jnp.int32, (n, n), 0)
    jj = lax.broadcasted_iota(jnp.int32, (n, n), 1)
    ts = None
    for level in range(int(math.log2(n))):
        rb = jnp.right_shift(ii, level)
        cb = jnp.right_shift(jj, level)
        sel = (jnp.bitwise_and(rb, 1) == 1) & (cb == rb - 1)
        if level == 0:
            eye = jnp.where(ii == jj, 1.0, 0.0)
            ts = [eye - jnp.where(sel, low, 0.0) for low in lows]
        else:
            t16 = [t.astype(BF16) for t in ts]
            ys = [_dot(jnp.where(sel, low, 0.0).astype(BF16), t).astype(BF16) for low, t in zip(lows, t16)]
            ts = [t - _dot(tb16, y) for t, tb16, y in zip(ts, t16, ys)]
    return ts


def _gdn_kernel(q_ref, k_ref, v_ref, z_ref, bg_ref, bgt_ref, nw_ref, o_ref, s_ref):
    ck = GDN_CHUNK
    hd = GDN_HEAD_DIM
    n_chunks = q_ref.shape[0] // ck

    @pl.when(pl.program_id(0) == 0)
    def _():
        s_ref[...] = jnp.zeros_like(s_ref)

    ii = lax.broadcasted_iota(jnp.int32, (ck, ck), 0)
    jj = lax.broadcasted_iota(jnp.int32, (ck, ck), 1)
    causal = ii >= jj
    strict = ii > jj
    nw = nw_ref[...]
    heads = range(GDN_HEADS)
    items = [(c, h) for c in range(n_chunks) for h in heads]
    rows = lambda c: slice(c * ck, (c + 1) * ck)
    cols = lambda h: slice(h * hd, (h + 1) * hd)

    q = [q_ref[rows(c), cols(h)] for c, h in items]
    k = [k_ref[rows(c), cols(h)] for c, h in items]
    beta = [bg_ref[rows(c), h:h + 1] for c, h in items]
    gc = [bg_ref[rows(c), GDN_HEADS + h:GDN_HEADS + h + 1] for c, h in items]
    gl = [bg_ref[rows(c), 2 * GDN_HEADS + h:2 * GDN_HEADS + h + 1] for c, h in items]
    gc_row = [bgt_ref[h:h + 1, rows(c)] for c, h in items]
    n_items = range(len(items))
    eg = [jnp.exp(gc[n]) for n in n_items]
    decay = [jnp.exp(jnp.where(causal, gc[n] - gc_row[n], -jnp.inf)) for n in n_items]
    kb = [k[n] * beta[n] for n in n_items]
    kq = [_dot_nt(jnp.concatenate([kb[n], q[n]], axis=0).astype(BF16), k[n].astype(BF16)) for n in n_items]
    low = [jnp.where(strict, kq[n][:ck] * decay[n], 0.0) for n in n_items]
    attn = [(kq[n][ck:] * decay[n]).astype(BF16) for n in n_items]
    t_inv = _tri_inverse_all(low)
    rhs = [jnp.concatenate([v_ref[rows(c), cols(h)] * beta[n], kb[n] * eg[n]], axis=1).astype(BF16)
           for n, (c, h) in enumerate(items)]
    uw = [_dot(t_inv[n].astype(BF16), rhs[n]) for n in n_items]
    wq = [jnp.concatenate([uw[n][:, hd:], q[n] * eg[n]], axis=0).astype(BF16) for n in n_items]
    k_dec_t = [(k[n] * jnp.exp(gl[n] - gc[n])).T.astype(BF16) for n in n_items]
    egl = [jnp.broadcast_to(jnp.exp(gl[n]), (ck, hd))[0:1, :] for n in n_items]

    state = [s_ref[h] for h in heads]
    for c in range(n_chunks):
        base = c * GDN_HEADS
        ws = [_dot(wq[base + h], state[h].astype(BF16)) for h in heads]
        v_new = [(uw[base + h][:, :hd] - ws[h][:ck]).astype(BF16) for h in heads]
        state = [state[h] * egl[base + h] + _dot(k_dec_t[base + h], v_new[h]) for h in heads]
        o = [ws[h][ck:] + _dot(attn[base + h], v_new[h]) for h in heads]
        o = [o[h] * lax.rsqrt(jnp.mean(o[h] * o[h], axis=-1, keepdims=True) + NORM_EPS) * nw for h in heads]
        o_ref[rows(c), :] = jnp.concatenate(
            [(o[h] * z_ref[rows(c), cols(h)]).astype(o_ref.dtype) for h in heads], axis=1)
    for h in heads:
        s_ref[h] = state[h]


def _gdn(qkv, z, bg, bgt, norm_w):
    t = qkv.shape[0]
    tb = TB_GDN
    gd, hd = GDN_DIM, GDN_HEAD_DIM
    col = lambda c: pl.BlockSpec((tb, gd), lambda i: (i, c))
    return pl.pallas_call(
        _gdn_kernel,
        grid=(t // tb,),
        in_specs=[col(0), col(1), col(2), col(0),
                  pl.BlockSpec((tb, LANES), lambda i: (i, 0)),
                  pl.BlockSpec((SUBLANES, tb), lambda i: (1, i)),
                  pl.BlockSpec((1, hd), lambda i: (0, 0))],
        out_specs=pl.BlockSpec((tb, gd), lambda i: (i, 0)),
        out_shape=jax.ShapeDtypeStruct((t, gd), BF16),
        scratch_shapes=[pltpu.VMEM((GDN_HEADS, hd, hd), F32)],
        compiler_params=_params("arbitrary"),
    )(qkv, qkv, qkv, z, bg, bgt, norm_w.reshape(1, hd))


def _rope_kernel(pos_ref, cos_ref, sin_ref):
    half = RET_QK_DIM // 2
    lane = lax.broadcasted_iota(jnp.int32, (1, RET_QK_DIM), 1)
    idx = jnp.where(lane < half, lane, lane - half).astype(F32)
    inv_freq = jnp.exp(idx * (-math.log(ROPE_BASE) / half))
    ang = pos_ref[...].astype(F32) * inv_freq
    cos_ref[...] = jnp.cos(ang)
    sin_ref[...] = jnp.where(lane < half, -1.0, 1.0) * jnp.sin(ang)


def _rope_tables(positions):
    t = positions.shape[0]
    tb = 1024
    return pl.pallas_call(
        _rope_kernel,
        grid=(t // tb,),
        in_specs=[pl.BlockSpec((tb, 1), lambda i: (i, 0))],
        out_specs=[pl.BlockSpec((tb, RET_QK_DIM), lambda i: (i, 0))] * 2,
        out_shape=[jax.ShapeDtypeStruct((t, RET_QK_DIM), F32)] * 2,
        compiler_params=_params("arbitrary"),
    )(positions.reshape(t, 1))


def _ret_kernel(q_ref, k_ref, v_ref, g_ref, cos_ref, sin_ref, o_ref, r_ref, dmask_ref, qs_ref, ks_ref):
    tb = q_ref.shape[0]
    qk, dv = RET_QK_DIM, RET_V_DIM
    heads = range(RET_HEADS)
    log_gamma = [math.log(1.0 - 2.0 ** (-RET_DECAY_BASE - h)) for h in heads]

    @pl.when(pl.program_id(0) == 0)
    def _():
        r_ref[...] = jnp.zeros_like(r_ref)
        ii = lax.broadcasted_iota(jnp.int32, (tb, tb), 0)
        jj = lax.broadcasted_iota(jnp.int32, (tb, tb), 1)
        rel = (ii - jj).astype(F32)
        idx = lax.broadcasted_iota(jnp.int32, (tb, qk), 0).astype(F32)
        for h in heads:
            dmask_ref[h] = jnp.where(rel >= 0.0, jnp.exp(jnp.maximum(rel, 0.0) * log_gamma[h]), 0.0)
            qs_ref[h] = jnp.exp((idx + 1.0) * log_gamma[h])
            ks_ref[h] = jnp.exp((tb - 1.0 - idx) * log_gamma[h])

    cos, sin = cos_ref[...], sin_ref[...]
    rope = lambda x: x * cos + pltpu.roll(x, qk // 2, axis=1) * sin
    q = [rope(q_ref[:, h * qk:(h + 1) * qk]) for h in heads]
    k = [rope(k_ref[:, h * qk:(h + 1) * qk]) * (qk ** -0.5) for h in heads]
    v16 = [v_ref[:, h * dv:(h + 1) * dv].astype(BF16) for h in heads]
    scores = [(_dot_nt(q[h].astype(BF16), k[h].astype(BF16)) * dmask_ref[h]).astype(BF16) for h in heads]
    state = [r_ref[h] for h in heads]
    o = [_dot(scores[h], v16[h]) + _dot((q[h] * qs_ref[h]).astype(BF16), state[h].astype(BF16)) for h in heads]
    for h in heads:
        r_ref[h] = state[h] * math.exp(tb * log_gamma[h]) + _dot((k[h] * ks_ref[h]).T.astype(BF16), v16[h])
    outs = []
    for h in heads:
        oc = o[h] - jnp.mean(o[h], axis=-1, keepdims=True)
        on = oc * lax.rsqrt(jnp.mean(oc * oc, axis=-1, keepdims=True) + LN_EPS)
        outs.append((on * _silu(g_ref[:, h * dv:(h + 1) * dv])).astype(o_ref.dtype))
    o_ref[...] = jnp.concatenate(outs, axis=1)


def _retention(rest, cos, sin):
    t = rest.shape[0]
    tb = TB_RET
    qk, dv, nh = RET_QK_DIM, RET_V_DIM, RET_HEADS
    return pl.pallas_call(
        _ret_kernel,
        grid=(t // tb,),
        in_specs=[pl.BlockSpec((tb, RET_QK), lambda i: (i, R_RQ // RET_QK)),
                  pl.BlockSpec((tb, RET_QK), lambda i: (i, R_RK // RET_QK)),
                  pl.BlockSpec((tb, RET_V), lambda i: (i, R_RV // RET_V)),
                  pl.BlockSpec((tb, RET_V), lambda i: (i, R_RG // RET_V)),
                  pl.BlockSpec((tb, qk), lambda i: (i, 0)),
                  pl.BlockSpec((tb, qk), lambda i: (i, 0))],
        out_specs=pl.BlockSpec((tb, RET_V), lambda i: (i, 0)),
        out_shape=jax.ShapeDtypeStruct((t, RET_V), BF16),
        scratch_shapes=[pltpu.VMEM((nh, qk, dv), F32),
                        pltpu.VMEM((nh, tb, tb), F32),
                        pltpu.VMEM((nh, tb, qk), F32),
                        pltpu.VMEM((nh, tb, qk), F32)],
        compiler_params=_params("arbitrary"),
    )(rest, rest, rest, rest, cos, sin)


def _conformer_kernel(ca_ref, cb_ref, w_ref, b_ref, lnw_ref, lnb_ref, o_ref, upad_ref, sh_ref):
    tb = ca_ref.shape[0]
    rows = tb + HALO

    @pl.when(pl.program_id(0) == 0)
    def _():
        upad_ref[0:HALO, :] = jnp.zeros((HALO, CONV_DIM), F32)

    upad_ref[HALO:, :] = ca_ref[...] * jax.nn.sigmoid(cb_ref[...])
    full = upad_ref[...]
    for r in range(1, SUBLANES):
        sh_ref[r - 1] = pltpu.roll(full, rows - r, axis=0)
    base = HALO - (CONV_WIDTH - 1)
    acc = None
    for j in range(CONV_WIDTH):
        off = base + j
        r, a = off % SUBLANES, off - off % SUBLANES
        src = upad_ref[a:a + tb, :] if r == 0 else sh_ref[r - 1, a:a + tb, :]
        term = w_ref[j:j + 1, :] * src
        acc = term if acc is None else acc + term
    upad_ref[0:HALO, :] = upad_ref[tb:tb + HALO, :]
    u = acc + b_ref[...]
    mu = jnp.mean(u, axis=-1, keepdims=True)
    uc = u - mu
    u = uc * lax.rsqrt(jnp.mean(uc * uc, axis=-1, keepdims=True) + LN_EPS) * lnw_ref[...] + lnb_ref[...]
    o_ref[...] = _silu(u).astype(o_ref.dtype)


def _conformer(rest, w, b, ln_w, ln_b):
    t = rest.shape[0]
    tb = TB_CONV
    c = CONV_DIM
    vec = lambda: pl.BlockSpec((1, c), lambda i: (0, 0))
    return pl.pallas_call(
        _conformer_kernel,
        grid=(t // tb,),
        in_specs=[pl.BlockSpec((tb, c), lambda i: (i, R_GLU // c)),
                  pl.BlockSpec((tb, c), lambda i: (i, R_GLU // c + 1)),
                  pl.BlockSpec((CONV_WIDTH, c), lambda i: (0, 0)),
                  vec(), vec(), vec()],
        out_specs=pl.BlockSpec((tb, c), lambda i: (i, 0)),
        out_shape=jax.ShapeDtypeStruct((t, c), BF16),
        scratch_shapes=[pltpu.VMEM((tb + HALO, c), F32),
                        pltpu.VMEM((SUBLANES - 1, tb + HALO, c), F32)],
        compiler_params=_params("arbitrary"),
    )(rest, rest, w, b.reshape(1, c), ln_w.reshape(1, c), ln_b.reshape(1, c))


def _merge_kernel(oa_ref, ob_ref, oc_ref, wa_ref, wb_ref, wc_ref, ga_ref, gb_ref, gc_ref, o_ref,
                  wa16_ref, wb16_ref, wc16_ref):
    @pl.when(pl.program_id(1) == 0)
    def _():
        wa16_ref[...] = wa_ref[...].astype(BF16)
        wb16_ref[...] = wb_ref[...].astype(BF16)
        wc16_ref[...] = wc_ref[...].astype(BF16)
    m = ga_ref[...].astype(F32) * _dot(oa_ref[...], wa16_ref[...])
    m = m + gb_ref[...].astype(F32) * _dot(ob_ref[...], wb16_ref[...])
    m = m + gc_ref[...].astype(F32) * _dot(oc_ref[...], wc16_ref[...])
    o_ref[...] = m.astype(o_ref.dtype)


def _merge(o_a, o_b, o_c, w_a, w_b, w_c, layer, gates):
    t, k = o_a.shape
    n = w_a.shape[2]
    tm, tn = min(TM_MERGE, t), TN_MERGE
    act = lambda: pl.BlockSpec((tm, k), lambda j, i: (i, 0))
    wsp = lambda: pl.BlockSpec((None, k, tn), lambda j, i: (layer, 0, j))
    gate = lambda b: pl.BlockSpec((tm, tn), lambda j, i: (i, b * n // tn + j))
    return pl.pallas_call(
        _merge_kernel,
        grid=(n // tn, t // tm),
        in_specs=[act(), act(), act(), wsp(), wsp(), wsp(), gate(0), gate(1), gate(2)],
        out_specs=pl.BlockSpec((tm, tn), lambda j, i: (i, j)),
        out_shape=jax.ShapeDtypeStruct((t, n), BF16),
        scratch_shapes=[pltpu.VMEM((k, tn), BF16)] * 3,
        compiler_params=_params("arbitrary", "arbitrary"),
    )(o_a, o_b, o_c, w_a, w_b, w_c, gates, gates, gates)


def kernel(x, c, positions, w_ada, b_ada, norm_mix_w, norm_mlp_w, w_in, conv_qkv_w, gdn_a_log, gdn_dt_bias,
           gdn_norm_w, conv_dw_w, conv_dw_b, conv_ln_w, conv_ln_b, w_branch_a, w_branch_b, w_branch_c,
           w_out, w_mlp_in, w_mlp_out, final_norm_w):
    bsz, t, d = x.shape
    assert bsz == 1
    xs = x.reshape(t, d)
    mod = _adaln_mod(c, w_ada, b_ada)
    cos, sin = _rope_tables(positions.reshape(t))
    w_in_t = jnp.swapaxes(w_in, 1, 2)
    for l in range(w_in.shape[0]):
        h = _norm_mod(xs, norm_mix_w[l], mod[l], 0, 1)
        qkv = _proj(h, w_in_t, l, 0, 3 * GDN_DIM, "conv_l2", conv_qkv_w[l], 0)
        z = _proj(h, w_in_t, l, 3 * GDN_DIM, GDN_DIM, "silu")
        rest = _proj(h, w_in_t, l, P_RQ + BA_COLS, R_GATE, "plain")
        gates = _proj(h, w_in_t, l, P_RQ + BA_COLS + R_GATE, R_WIDTH - R_GATE, "sigmoid", out_dtype=BF16)
        bg, bgt = _gdn_gates(h, w_in_t, l, gdn_a_log[l], gdn_dt_bias[l])
        o_a = _gdn(qkv, z, bg, bgt, gdn_norm_w[l])
        o_b = _retention(rest, cos, sin)
        o_c = _conformer(rest, conv_dw_w[l], conv_dw_b[l], conv_ln_w[l], conv_ln_b[l])
        merged = _merge(o_a, o_b, o_c, w_branch_a, w_branch_b, w_branch_c, l, gates)
        xs, h = _out_proj_norm(merged, w_out, l, xs, mod[l, 2:3], norm_mlp_w[l], mod[l], 3, 4)
        act = _matmul(h, w_mlp_in, l, tm=TM_WIDE, out_dtype=BF16, relu2=True)
        xs = _mlp_out(act, w_mlp_out, l, xs, mod[l, 5:6])
    return _final_norm(xs, final_norm_w).reshape(bsz, t, d)
```
